```python
import jax, jax.numpy as jnp
from jax import lax
import numpy as np

D_MODEL = 2048
BATCH = 4
SEQ = 4096
DEPTH = 4

CHUNK = 64
EPS = 1e-6
ROPE_BASE = 10000.0
MLA_HEADS = 16
MLA_NOPE = 128
MLA_ROPE = 64
MLA_V = 128
Q_LORA = 512
KV_LORA = 512
Q_BLOCK = 128
RET_HEADS = 8
RET_QK = 256
RET_V = 256
RET_GN_EPS = 1e-5
GM_GROUPS = 4
GM_WIDTH = 2048
GM_BLOCK = 128
BRANCH_W = 2048
N_BRANCH = 3
D_FF = 5504
IN_SIZES = (Q_LORA, KV_LORA, MLA_ROPE, RET_HEADS * RET_QK, RET_HEADS * RET_QK, RET_HEADS * RET_V, RET_HEADS * RET_V, GM_WIDTH, GM_WIDTH)
IN_SPLITS = tuple(int(v) for v in np.cumsum(IN_SIZES)[:-1])
D_IN = int(sum(IN_SIZES))

kernel_name = 'hybrid_mla_retention_gmlp_macaron'


def rms_norm(x, g):
    x32 = x.astype(jnp.float32)
    y = x32 * lax.rsqrt(jnp.mean(x32 * x32, axis=-1, keepdims=True) + EPS)
    return (y * g.astype(jnp.float32)).astype(x.dtype)


def layer_norm(x, g, b):
    x32 = x.astype(jnp.float32)
    mu = jnp.mean(x32, axis=-1, keepdims=True)
    var = jnp.mean(jnp.square(x32 - mu), axis=-1, keepdims=True)
    y = (x32 - mu) * lax.rsqrt(var + EPS)
    return (y * g.astype(jnp.float32) + b.astype(jnp.float32)).astype(x.dtype)


def rope_tables(pos, dim):
    inv = ROPE_BASE ** (-jnp.arange(0, dim, 2, dtype=jnp.float32) / dim)
    ang = pos.astype(jnp.float32)[..., None] * inv
    return jnp.cos(ang), jnp.sin(ang)


def apply_rope(x, cos, sin):
    x32 = x.astype(jnp.float32)
    x1, x2 = jnp.split(x32, 2, axis=-1)
    c = cos[:, :, None, :]
    s = sin[:, :, None, :]
    return jnp.concatenate([x1 * c - x2 * s, x2 * c + x1 * s], axis=-1).astype(x.dtype)


def swiglu(h, wi, wo):
    a, b = jnp.split(h @ wi, 2, axis=-1)
    return (jax.nn.silu(a) * b) @ wo


def mla_branch(z_q, z_kv, z_kr, q_norm_g, w_uq, kv_norm_g, w_ukv, cos_r, sin_r):
    B, S, _ = z_q.shape
    q = (rms_norm(z_q, q_norm_g) @ w_uq).reshape(B, S, MLA_HEADS, MLA_NOPE + MLA_ROPE)
    q_nope = q[..., :MLA_NOPE]
    q_rope = apply_rope(q[..., MLA_NOPE:], cos_r, sin_r)
    kv = (rms_norm(z_kv, kv_norm_g) @ w_ukv).reshape(B, S, MLA_HEADS, MLA_NOPE + MLA_V)
    k_nope = kv[..., :MLA_NOPE]
    v = kv[..., MLA_NOPE:]
    k_rope = apply_rope(z_kr[:, :, None, :], cos_r, sin_r)[:, :, 0, :]
    scale = (MLA_NOPE + MLA_ROPE) ** -0.5
    nb = S // Q_BLOCK
    key_chunk = jnp.arange(S) // CHUNK

    def block(args):
        qn, qr, bi = args
        s = jnp.einsum('bqhd,bkhd->bhqk', qn, k_nope) + jnp.einsum('bqhr,bkr->bhqk', qr, k_rope)
        s = s.astype(jnp.float32) * scale
        q_chunk = (bi * Q_BLOCK + jnp.arange(Q_BLOCK)) // CHUNK
        mask = key_chunk[None, :] <= q_chunk[:, None]
        p = jax.nn.softmax(jnp.where(mask, s, -1e30), axis=-1).astype(v.dtype)
        return jnp.einsum('bhqk,bkhe->bqhe', p, v)

    qn_b = q_nope.reshape(B, nb, Q_BLOCK, MLA_HEADS, MLA_NOPE).transpose(1, 0, 2, 3, 4)
    qr_b = q_rope.reshape(B, nb, Q_BLOCK, MLA_HEADS, MLA_ROPE).transpose(1, 0, 2, 3, 4)
    o = lax.map(block, (qn_b, qr_b, jnp.arange(nb)))
    return o.transpose(1, 0, 2, 3, 4).reshape(B, S, MLA_HEADS * MLA_V)


def retention_branch(z_q, z_k, z_v, z_g, cos_k, sin_k):
    B, S, _ = z_q.shape
    f32 = jnp.float32
    q = apply_rope(z_q.reshape(B, S, RET_HEADS, RET_QK), cos_k, sin_k).astype(f32)
    k = apply_rope(z_k.reshape(B, S, RET_HEADS, RET_QK), cos_k, sin_k).astype(f32) * RET_QK ** -0.5
    v = z_v.reshape(B, S, RET_HEADS, RET_V).astype(f32)
    log_g = jnp.log1p(-(2.0 ** (-5.0 - jnp.arange(RET_HEADS, dtype=f32))))
    idx = jnp.arange(CHUNK, dtype=f32)
    decay_intra = jnp.exp(jnp.abs(idx[:, None] - idx[None, :])[None] * log_g[:, None, None])
    q_decay = jnp.exp((idx[:, None] + 1.0) * log_g[None, :])
    k_decay = jnp.exp((CHUNK - 1.0 - idx)[:, None] * log_g[None, :])
    chunk_decay = jnp.exp(CHUNK * log_g)
    nc = S // CHUNK

    def to_chunks(t):
        return t.reshape(B, nc, CHUNK, RET_HEADS, t.shape[-1]).transpose(1, 0, 2, 3, 4)

    def step(state, inp):
        qc, kc, vc = inp
        s = jnp.einsum('bihd,bjhd->bhij', qc, kc) * decay_intra
        o = jnp.einsum('bhij,bjhe->bihe', s, vc)
        o = o + jnp.einsum('bihd,bhde->bihe', qc * q_decay[None, :, :, None], state)
        state = state * chunk_decay[None, :, None, None] + jnp.einsum('bjhd,bjhe->bhde', kc * k_decay[None, :, :, None], vc)
        return state, o

    state0 = jnp.zeros((B, RET_HEADS, RET_QK, RET_V), f32)
    _, o = lax.scan(step, state0, (to_chunks(q), to_chunks(k), to_chunks(v)))
    o = o.transpose(1, 0, 2, 3, 4).reshape(B, S, RET_HEADS, RET_V)
    mu = jnp.mean(o, axis=-1, keepdims=True)
    var = jnp.mean(jnp.square(o - mu), axis=-1, keepdims=True)
    o = ((o - mu) * lax.rsqrt(var + RET_GN_EPS)).reshape(B, S, RET_HEADS * RET_V)
    return (jax.nn.silu(z_g.astype(f32)) * o).astype(z_g.dtype)


def gmlp_branch(z_u, z_v, ln_g, ln_b, w_s, b_s):
    B, S, _ = z_u.shape
    u = jax.nn.gelu(z_u)
    v = layer_norm(jax.nn.gelu(z_v), ln_g, ln_b)
    nb = S // GM_BLOCK
    vb = v.reshape(B, nb, GM_BLOCK, GM_GROUPS, GM_WIDTH // GM_GROUPS)
    pc = jnp.arange(GM_BLOCK) // CHUNK
    mask = pc[:, None] >= pc[None, :]
    w = jnp.where(mask[None], w_s, 0.0).astype(v.dtype)
    mixed = jnp.einsum('gij,bnjgc->bnigc', w, vb) + b_s.T[None, None, :, :, None]
    return u * mixed.reshape(B, S, GM_WIDTH)


def token_mixer(h, w_in, q_norm_g, w_uq, kv_norm_g, w_ukv, gm_ln_g, gm_ln_b, gm_w_s, gm_b_s, w_gate, b_gate, w_br, w_o, cos_r, sin_r, cos_k, sin_k):
    B, S, D = h.shape
    z = h @ w_in
    zq, zkv, zkr, rq, rk, rv, rg, gu, gv = jnp.split(z, IN_SPLITS, axis=-1)
    y_a = mla_branch(zq, zkv, zkr, q_norm_g, w_uq, kv_norm_g, w_ukv, cos_r, sin_r)
    y_b = retention_branch(rq, rk, rv, rg, cos_k, sin_k)
    y_c = gmlp_branch(gu, gv, gm_ln_g, gm_ln_b, gm_w_s, gm_b_s)
    gates = jax.nn.sigmoid(h @ w_gate + b_gate).reshape(B, S, N_BRANCH, D)
    merged = gates[:, :, 0] * (y_a @ w_br[0]) + gates[:, :, 1] * (y_b @ w_br[1]) + gates[:, :, 2] * (y_c @ w_br[2])
    return merged @ w_o


def setup_inputs(seed: int = 0) -> dict:
    key = jax.random.key(seed)
    k = jax.random.split(key, 26)
    f32 = jnp.float32
    L = DEPTH

    def w(kk, shape, fan_in):
        return jax.random.normal(kk, shape, f32) * fan_in ** -0.5

    def gain(kk, shape):
        return 1.0 + 0.05 * jax.random.normal(kk, shape, f32)

    def small(kk, shape):
        return 0.01 * jax.random.normal(kk, shape, f32)

    x = jax.random.normal(k[0], (BATCH, SEQ, D_MODEL), f32)
    offset = jax.random.randint(k[1], (BATCH, 1), 0, 1024) * CHUNK
    pos = (offset + jnp.arange(SEQ)[None, :]).astype(jnp.int32)
    return {
        'x': x,
        'pos': pos,
        'ffn1_pre_g': gain(k[2], (L, D_MODEL)),
        'ffn1_wi': w(k[3], (L, D_MODEL, 2 * D_FF), D_MODEL),
        'ffn1_wo': w(k[4], (L, D_FF, D_MODEL), D_FF),
        'ffn1_post_g': gain(k[5], (L, D_MODEL)),
        'mix_pre_g': gain(k[6], (L, D_MODEL)),
        'w_in': w(k[7], (L, D_MODEL, D_IN), D_MODEL),
        'q_norm_g': gain(k[8], (L, Q_LORA)),
        'w_uq': w(k[9], (L, Q_LORA, MLA_HEADS * (MLA_NOPE + MLA_ROPE)), Q_LORA),
        'kv_norm_g': gain(k[10], (L, KV_LORA)),
        'w_ukv': w(k[11], (L, KV_LORA, MLA_HEADS * (MLA_NOPE + MLA_V)), KV_LORA),
        'gm_ln_g': gain(k[12], (L, GM_WIDTH)),
        'gm_ln_b': small(k[13], (L, GM_WIDTH)),
        'gm_w_s': w(k[14], (L, GM_GROUPS, GM_BLOCK, GM_BLOCK), GM_BLOCK),
        'gm_b_s': 1.0 + 0.1 * jax.random.normal(k[15], (L, GM_GROUPS, GM_BLOCK), f32),
        'w_gate': w(k[16], (L, D_MODEL, N_BRANCH * D_MODEL), D_MODEL),
        'b_gate': small(k[17], (L, N_BRANCH * D_MODEL)),
        'w_br': w(k[18], (L, N_BRANCH, BRANCH_W, D_MODEL), BRANCH_W),
        'w_o': w(k[19], (L, D_MODEL, D_MODEL), D_MODEL),
        'mix_post_g': gain(k[20], (L, D_MODEL)),
        'ffn2_pre_g': gain(k[21], (L, D_MODEL)),
        'ffn2_wi': w(k[22], (L, D_MODEL, 2 * D_FF), D_MODEL),
        'ffn2_wo': w(k[23], (L, D_FF, D_MODEL), D_FF),
        'ffn2_post_g': gain(k[24], (L, D_MODEL)),
    }


def reference(x, pos, ffn1_pre_g, ffn1_wi, ffn1_wo, ffn1_post_g, mix_pre_g, w_in, q_norm_g, w_uq, kv_norm_g, w_ukv, gm_ln_g, gm_ln_b, gm_w_s, gm_b_s, w_gate, b_gate, w_br, w_o, mix_post_g, ffn2_pre_g, ffn2_wi, ffn2_wo, ffn2_post_g):
    cos_r, sin_r = rope_tables(pos, MLA_ROPE)
    cos_k, sin_k = rope_tables(pos, RET_QK)
    for l in range(DEPTH):
        x = x + 0.5 * rms_norm(swiglu(rms_norm(x, ffn1_pre_g[l]), ffn1_wi[l], ffn1_wo[l]), ffn1_post_g[l])
        m = token_mixer(rms_norm(x, mix_pre_g[l]), w_in[l], q_norm_g[l], w_uq[l], kv_norm_g[l], w_ukv[l], gm_ln_g[l], gm_ln_b[l], gm_w_s[l], gm_b_s[l], w_gate[l], b_gate[l], w_br[l], w_o[l], cos_r, sin_r, cos_k, sin_k)
        x = x + rms_norm(m, mix_post_g[l])
        x = x + 0.5 * rms_norm(swiglu(rms_norm(x, ffn2_pre_g[l]), ffn2_wi[l], ffn2_wo[l]), ffn2_post_g[l])
    return x
```

```python
import functools

import numpy as np
import jax
import jax.numpy as jnp
from jax import lax
from jax.experimental import pallas as pl
from jax.experimental.pallas import tpu as pltpu

F32 = jnp.float32
BF16 = jnp.bfloat16

D_MODEL = 2048
DEPTH = 4
CHUNK = 64
EPS = 1e-6
ROPE_BASE = 10000.0
MLA_HEADS = 16
MLA_NOPE = 128
MLA_ROPE = 64
MLA_V = 128
Q_LORA = 512
KV_LORA = 512
RET_HEADS = 8
RET_QK = 256
RET_V = 256
RET_GN_EPS = 1e-5
GM_GROUPS = 4
GM_WIDTH = 2048
GM_BLOCK = 128
N_BRANCH = 3
D_FF = 5504

LANE = 128
MLA_QK_PAD = 2 * LANE
MLA_GROUP_W = Q_LORA + KV_LORA + LANE
RET_GROUP_W = 4 * RET_HEADS * RET_QK
GM_GROUP_W = 2 * GM_WIDTH
FF_TILE = 512
D_FF_PAD = ((D_FF + FF_TILE - 1) // FF_TILE) * FF_TILE
VMEM_LIMIT = 56 * 2 ** 20


def _params(*sem):
    return pltpu.CompilerParams(dimension_semantics=sem, vmem_limit_bytes=VMEM_LIMIT)


def _rms(x, g):
    return x * lax.rsqrt(jnp.mean(x * x, axis=-1, keepdims=True) + EPS) * g


def _rope_table_kernel(ang_r_ref, ang_k_ref, cr_ref, sr_ref, ck_ref, sk_ref):
    a = ang_r_ref[...]
    live = lax.broadcasted_iota(jnp.int32, a.shape, 1) < MLA_ROPE
    cr_ref[...] = jnp.where(live, jnp.cos(a), 0.0)
    sr_ref[...] = jnp.where(live, jnp.sin(a), 0.0)
    k = ang_k_ref[...]
    ck_ref[...] = jnp.cos(k)
    sk_ref[...] = jnp.sin(k)


def _rope_tables(pos):
    n = pos.size
    p = pos.astype(F32).reshape(n, 1)
    inv_r = ROPE_BASE ** (-jnp.arange(0, MLA_ROPE, 2, dtype=F32) / MLA_ROPE)
    inv_k = ROPE_BASE ** (-jnp.arange(0, RET_QK, 2, dtype=F32) / RET_QK)
    ang_r = p * inv_r
    ang_r = jnp.concatenate([ang_r, ang_r, jnp.zeros((n, LANE - MLA_ROPE), F32)], axis=1)
    ang_k = p * inv_k
    tm = min(n, 1024)
    spec = pl.BlockSpec((tm, LANE), lambda i: (i, 0))
    out = jax.ShapeDtypeStruct((n, LANE), F32)
    return pl.pallas_call(
        _rope_table_kernel,
        grid=(n // tm,),
        in_specs=[spec, spec],
        out_specs=[spec] * 4,
        out_shape=[out] * 4,
        compiler_params=_params("parallel"),
        name="rope_tables",
    )(ang_r, ang_k)


def _ffn_kernel(x_ref, gpre_ref, wa_ref, wb_ref, wo_ref, gpost_ref, o_ref, h_ref, acc_ref):
    f = pl.program_id(1)

    @pl.when(f == 0)
    def _():
        h_ref[...] = _rms(x_ref[...], gpre_ref[...]).astype(BF16)
        acc_ref[...] = jnp.zeros_like(acc_ref)

    h = h_ref[...]
    a = jnp.dot(h, wa_ref[...], preferred_element_type=F32)
    b = jnp.dot(h, wb_ref[...], preferred_element_type=F32)
    act = (a * jax.nn.sigmoid(a) * b).astype(BF16)
    acc_ref[...] += jnp.dot(act, wo_ref[...], preferred_element_type=F32)

    @pl.when(f == pl.num_programs(1) - 1)
    def _():
        o_ref[...] = x_ref[...] + 0.5 * _rms(acc_ref[...], gpost_ref[...])


def _ffn(x, g_pre, wi, wo, g_post, tm=512):
    n, d = x.shape
    nf = D_FF_PAD // FF_TILE
    tm = min(tm, n)
    return pl.pallas_call(
        _ffn_kernel,
        grid=(n // tm, nf),
        in_specs=[
            pl.BlockSpec((tm, d), lambda i, f: (i, 0)),
            pl.BlockSpec((1, d), lambda i, f: (0, 0)),
            pl.BlockSpec((d, FF_TILE), lambda i, f: (0, f)),
            pl.BlockSpec((d, FF_TILE), lambda i, f: (0, f + nf)),
            pl.BlockSpec((FF_TILE, d), lambda i, f: (f, 0)),
            pl.BlockSpec((1, d), lambda i, f: (0, 0)),
        ],
        out_specs=pl.BlockSpec((tm, d), lambda i, f: (i, 0)),
        out_shape=jax.ShapeDtypeStruct((n, d), F32),
        scratch_shapes=[pltpu.VMEM((tm, d), BF16), pltpu.VMEM((tm, d), F32)],
        compiler_params=_params("parallel", "arbitrary"),
        name="ffn",
    )(x, g_pre, wi, wi, wo, g_post)


def _prenorm_kernel(x_ref, g_ref, o_ref):
    o_ref[...] = _rms(x_ref[...], g_ref[...]).astype(BF16)


def _prenorm(x, g, tm=1024):
    n, d = x.shape
    tm = min(tm, n)
    return pl.pallas_call(
        _prenorm_kernel,
        grid=(n // tm,),
        in_specs=[pl.BlockSpec((tm, d), lambda i: (i, 0)), pl.BlockSpec((1, d), lambda i: (0, 0))],
        out_specs=pl.BlockSpec((tm, d), lambda i: (i, 0)),
        out_shape=jax.ShapeDtypeStruct((n, d), BF16),
        compiler_params=_params("parallel"),
        name="prenorm",
    )(x, g)


def _mm_kernel(a_ref, w_ref, o_ref):
    o_ref[...] = jnp.dot(a_ref[...], w_ref[...], preferred_element_type=F32).astype(o_ref.dtype)


def _mm(a, w, tn, tm=1024):
    n, k = a.shape
    nc = w.shape[1]
    tm = min(tm, n)
    return pl.pallas_call(
        _mm_kernel,
        grid=(n // tm, nc // tn),
        in_specs=[pl.BlockSpec((tm, k), lambda i, j: (i, 0)), pl.BlockSpec((k, tn), lambda i, j: (0, j))],
        out_specs=pl.BlockSpec((tm, tn), lambda i, j: (i, j)),
        out_shape=jax.ShapeDtypeStruct((n, nc), BF16),
        compiler_params=_params("parallel", "arbitrary"),
        name="in_proj",
    )(a, w)


def _rope_half_block(blk, c, s):
    return blk * c + pltpu.roll(blk, MLA_ROPE, axis=1) * s


def _mla_prep_kernel(z_ref, gq_ref, gkv_ref, wq_ref, wkv_ref, c_ref, s_ref, q_ref, kv_ref, kr_ref):
    c = c_ref[...]
    s = s_ref[...]
    hq = _rms(z_ref[:, :Q_LORA].astype(F32), gq_ref[...]).astype(BF16)
    for h in range(MLA_HEADS):
        lo = h * MLA_QK_PAD
        qh = jnp.dot(hq, wq_ref[:, lo:lo + MLA_QK_PAD], preferred_element_type=F32)
        q_ref[:, lo:lo + LANE] = qh[:, :LANE].astype(BF16)
        q_ref[:, lo + LANE:lo + MLA_QK_PAD] = _rope_half_block(qh[:, LANE:], c, s).astype(BF16)
    hkv = _rms(z_ref[:, Q_LORA:Q_LORA + KV_LORA].astype(F32), gkv_ref[...]).astype(BF16)
    step = 4 * (MLA_NOPE + MLA_V)
    for lo in range(0, MLA_HEADS * (MLA_NOPE + MLA_V), step):
        kv_ref[:, lo:lo + step] = jnp.dot(hkv, wkv_ref[:, lo:lo + step], preferred_element_type=F32).astype(BF16)
    kr_ref[...] = _rope_half_block(z_ref[:, Q_LORA + KV_LORA:].astype(F32), c, s).astype(BF16)


def _mla_prep(zm, gq, gkv, wq, wkv, c_r, s_r, tm=256):
    n = zm.shape[0]
    tm = min(tm, n)
    wq_w = MLA_HEADS * MLA_QK_PAD
    wkv_w = MLA_HEADS * (MLA_NOPE + MLA_V)
    row = lambda w: pl.BlockSpec((tm, w), lambda i: (i, 0))
    full = lambda r, w: pl.BlockSpec((r, w), lambda i: (0, 0))
    return pl.pallas_call(
        _mla_prep_kernel,
        grid=(n // tm,),
        in_specs=[row(MLA_GROUP_W), full(1, Q_LORA), full(1, KV_LORA), full(Q_LORA, wq_w), full(KV_LORA, wkv_w),
                  row(LANE), row(LANE)],
        out_specs=[row(wq_w), row(wkv_w), row(LANE)],
        out_shape=[jax.ShapeDtypeStruct((n, wq_w), BF16), jax.ShapeDtypeStruct((n, wkv_w), BF16),
                   jax.ShapeDtypeStruct((n, LANE), BF16)],
        compiler_params=_params("parallel"),
        name="mla_prep",
    )(zm, gq, gkv, wq, wkv, c_r, s_r)


def _attn_kernel(q_ref, kv_ref, kr_ref, o_ref, *, t):
    i = pl.program_id(2)
    q = q_ref[...]
    scale = (MLA_NOPE + MLA_ROPE) ** -0.5

    def scores(k0):
        k = jnp.concatenate([kv_ref[pl.ds(k0, t), :MLA_NOPE], kr_ref[pl.ds(k0, t), :]], axis=1)
        s = lax.dot_general(q, k, (((1,), (1,)), ((), ())), preferred_element_type=F32)
        return s * scale

    def update(carry, s, k0):
        m, l, acc = carry
        m_new = jnp.maximum(m, jnp.max(s, axis=-1, keepdims=True))
        alpha = jnp.exp(m - m_new)
        p = jnp.exp(s - m_new)
        l = alpha * l + jnp.sum(p, axis=-1, keepdims=True)
        v = kv_ref[pl.ds(k0, t), MLA_NOPE:]
        acc = alpha * acc + jnp.dot(p.astype(BF16), v, preferred_element_type=F32)
        return m_new, l, acc

    def body(kb, carry):
        k0 = pl.multiple_of(kb * t, t)
        return update(carry, scores(k0), k0)

    init = (jnp.full((t, 1), -1e30, F32), jnp.zeros((t, 1), F32), jnp.zeros((t, MLA_V), F32))
    carry = lax.fori_loop(0, i, body, init)
    k0 = pl.multiple_of(i * t, t)
    qc = lax.broadcasted_iota(jnp.int32, (t, t), 0) // CHUNK
    kc = lax.broadcasted_iota(jnp.int32, (t, t), 1) // CHUNK
    s = jnp.where(kc <= qc, scores(k0), -1e30)
    m, l, acc = update(carry, s, k0)
    o_ref[...] = (acc / l).astype(BF16)


def _attention(q, kv, kr, b, s, t=256):
    n = q.shape[0]
    t = min(t, s)
    nq = s // t
    kv3 = kv.reshape(b, s, kv.shape[1])
    kr3 = kr.reshape(b, s, LANE)
    return pl.pallas_call(
        functools.partial(_attn_kernel, t=t),
        grid=(b, MLA_HEADS, nq),
        in_specs=[
            pl.BlockSpec((t, MLA_QK_PAD), lambda bi, h, i: (bi * nq + i, h)),
            pl.BlockSpec((None, s, MLA_NOPE + MLA_V), lambda bi, h, i: (bi, 0, h)),
            pl.BlockSpec((None, s, LANE), lambda bi, h, i: (bi, 0, 0)),
        ],
        out_specs=pl.BlockSpec((t, MLA_V), lambda bi, h, i: (bi * nq + i, h)),
        out_shape=jax.ShapeDtypeStruct((n, MLA_HEADS * MLA_V), BF16),
        compiler_params=_params("parallel", "parallel", "arbitrary"),
        name="mla_attention",
    )(q, kv3, kr3)


def _ret_kernel(lg_ref, q_ref, k_ref, v_ref, g_ref, c_ref, s_ref, o_ref, state_ref, *, t):
    h = pl.program_id(1)
    lg = lg_ref[h]

    @pl.when(pl.program_id(2) == 0)
    def _():
        state_ref[...] = jnp.zeros_like(state_ref)

    c = c_ref[...]
    s = s_ref[...]
    half = RET_QK // 2

    def rope(x):
        x1 = x[:, :half]
        x2 = x[:, half:]
        return jnp.concatenate([x1 * c - x2 * s, x2 * c + x1 * s], axis=1)

    q = rope(q_ref[...].astype(F32))
    k = rope(k_ref[...].astype(F32)) * RET_QK ** -0.5
    v = v_ref[...]
    pos = lax.broadcasted_iota(jnp.int32, (t, 1), 0).astype(F32)
    q_dec = q * jnp.exp((pos + 1.0) * lg)
    k_dec = k * jnp.exp((t - 1.0 - pos) * lg)

    ii = lax.broadcasted_iota(jnp.int32, (t, t), 0)
    jj = lax.broadcasted_iota(jnp.int32, (t, t), 1)
    dec = jnp.where(jj // CHUNK <= ii // CHUNK, jnp.exp(jnp.abs(ii - jj).astype(F32) * lg), 0.0)
    a = lax.dot_general(q.astype(BF16), k.astype(BF16), (((1,), (1,)), ((), ())), preferred_element_type=F32) * dec
    state = state_ref[...]
    o = jnp.dot(a.astype(BF16), v, preferred_element_type=F32)
    o = o + jnp.dot(q_dec.astype(BF16), state.astype(BF16), preferred_element_type=F32)
    block_decay = jnp.exp(jnp.full((1, RET_V), t * 1.0, F32) * lg)
    state_ref[...] = state * block_decay + lax.dot_general(
        k_dec.astype(BF16), v, (((0,), (0,)), ((), ())), preferred_element_type=F32)

    mu = jnp.mean(o, axis=-1, keepdims=True)
    d = o - mu
    var = jnp.mean(d * d, axis=-1, keepdims=True)
    on = d * lax.rsqrt(var + RET_GN_EPS)
    g = g_ref[...].astype(F32)
    o_ref[...] = (g * jax.nn.sigmoid(g) * on).astype(BF16)


def _retention(zr, c_k, s_k, b, s, t=256):
    n = zr.shape[0]
    t = min(t, s)
    nt = s // t
    log_g = jnp.log1p(-(2.0 ** (-5.0 - jnp.arange(RET_HEADS, dtype=F32))))
    col = lambda part: pl.BlockSpec((t, RET_QK), lambda bi, h, ti: (bi * nt + ti, part * RET_HEADS + h))
    tab = pl.BlockSpec((t, LANE), lambda bi, h, ti: (bi * nt + ti, 0))
    return pl.pallas_call(
        functools.partial(_ret_kernel, t=t),
        grid=(b, RET_HEADS, nt),
        in_specs=[pl.BlockSpec(memory_space=pltpu.SMEM), col(0), col(1), col(2), col(3), tab, tab],
        out_specs=pl.BlockSpec((t, RET_V), lambda bi, h, ti: (bi * nt + ti, h)),
        out_shape=jax.ShapeDtypeStruct((n, RET_HEADS * RET_V), BF16),
        scratch_shapes=[pltpu.VMEM((RET_QK, RET_V), F32)],
        compiler_params=_params("parallel", "parallel", "arbitrary"),
        name="retention",
    )(log_g, zr, zr, zr, zr, c_k, s_k)


def _gmlp_kernel(u_ref, v_ref, lng_ref, lnb_ref, ws_ref, bst_ref, o_ref, *, nblk):
    v = jax.nn.gelu(v_ref[...].astype(F32))
    mu = jnp.mean(v, axis=-1, keepdims=True)
    d = v - mu
    var = jnp.mean(d * d, axis=-1, keepdims=True)
    vn = (d * lax.rsqrt(var + EPS) * lng_ref[...] + lnb_ref[...]).astype(BF16)
    pc_i = lax.broadcasted_iota(jnp.int32, (GM_BLOCK, GM_BLOCK), 0) // CHUNK
    pc_j = lax.broadcasted_iota(jnp.int32, (GM_BLOCK, GM_BLOCK), 1) // CHUNK
    gw = GM_WIDTH // GM_GROUPS
    for g in range(GM_GROUPS):
        w = jnp.where(pc_i >= pc_j, ws_ref[g], 0.0).astype(BF16)
        bias = bst_ref[:, g:g + 1]
        for r in range(nblk):
            rows = slice(r * GM_BLOCK, (r + 1) * GM_BLOCK)
            cols = slice(g * gw, (g + 1) * gw)
            mixed = jnp.dot(w, vn[rows, cols], preferred_element_type=F32) + bias
            u = jax.nn.gelu(u_ref[rows, cols].astype(F32))
            o_ref[rows, cols] = (u * mixed).astype(BF16)


def _gmlp(zg, ln_g, ln_b, w_s, b_s_t, nblk=2):
    n = zg.shape[0]
    tm = nblk * GM_BLOCK
    return pl.pallas_call(
        functools.partial(_gmlp_kernel, nblk=nblk),
        grid=(n // tm,),
        in_specs=[
            pl.BlockSpec((tm, GM_WIDTH), lambda i: (i, 0)),
            pl.BlockSpec((tm, GM_WIDTH), lambda i: (i, 1)),
            pl.BlockSpec((1, GM_WIDTH), lambda i: (0, 0)),
            pl.BlockSpec((1, GM_WIDTH), lambda i: (0, 0)),
            pl.BlockSpec((GM_GROUPS, GM_BLOCK, GM_BLOCK), lambda i: (0, 0, 0)),
            pl.BlockSpec((GM_BLOCK, GM_GROUPS), lambda i: (0, 0)),
        ],
        out_specs=pl.BlockSpec((tm, GM_WIDTH), lambda i: (i, 0)),
        out_shape=jax.ShapeDtypeStruct((n, GM_WIDTH), BF16),
        compiler_params=_params("parallel"),
        name="gmlp",
    )(zg, zg, ln_g, ln_b, w_s, b_s_t)


def _merge_kernel(h_ref, ya_ref, yb_ref, yc_ref, wg0_ref, wg1_ref, wg2_ref, bg0_ref, bg1_ref, bg2_ref,
                  wb0_ref, wb1_ref, wb2_ref, o_ref):
    h = h_ref[...]

    def branch(y_ref, wg_ref, bg_ref, wb_ref):
        gate = jax.nn.sigmoid(jnp.dot(h, wg_ref[...], preferred_element_type=F32) + bg_ref[...])
        return gate * jnp.dot(y_ref[...], wb_ref[...], preferred_element_type=F32)

    merged = branch(ya_ref, wg0_ref, bg0_ref, wb0_ref)
    merged = merged + branch(yb_ref, wg1_ref, bg1_ref, wb1_ref)
    merged = merged + branch(yc_ref, wg2_ref, bg2_ref, wb2_ref)
    o_ref[...] = merged.astype(BF16)


def _merge(h, ya, yb, yc, w_gate, b_gate, w_br, tm=512, tn=256):
    n, d = h.shape
    tm = min(tm, n)
    nj = d // tn
    act = pl.BlockSpec((tm, d), lambda i, j: (i, 0))
    wg = lambda br: pl.BlockSpec((d, tn), lambda i, j: (0, br * nj + j))
    bg = lambda br: pl.BlockSpec((1, tn), lambda i, j: (0, br * nj + j))
    wb = lambda br: pl.BlockSpec((None, d, tn), lambda i, j: (br, 0, j))
    return pl.pallas_call(
        _merge_kernel,
        grid=(n // tm, nj),
        in_specs=[act, act, act, act, wg(0), wg(1), wg(2), bg(0), bg(1), bg(2), wb(0), wb(1), wb(2)],
        out_specs=pl.BlockSpec((tm, tn), lambda i, j: (i, j)),
        out_shape=jax.ShapeDtypeStruct((n, d), BF16),
        compiler_params=_params("parallel", "arbitrary"),
        name="merge",
    )(h, ya, yb, yc, w_gate, w_gate, w_gate, b_gate, b_gate, b_gate, w_br, w_br, w_br)


def _out_proj_kernel(m_ref, w_ref, g_ref, x_ref, o_ref):
    y = jnp.dot(m_ref[...], w_ref[...], preferred_element_type=F32)
    o_ref[...] = x_ref[...] + _rms(y, g_ref[...])


def _out_proj(merged, w_o, g_post, x, tm=512):
    n, d = x.shape
    tm = min(tm, n)
    return pl.pallas_call(
        _out_proj_kernel,
        grid=(n // tm,),
        in_specs=[
            pl.BlockSpec((tm, d), lambda i: (i, 0)),
            pl.BlockSpec((d, d), lambda i: (0, 0)),
            pl.BlockSpec((1, d), lambda i: (0, 0)),
            pl.BlockSpec((tm, d), lambda i: (i, 0)),
        ],
        out_specs=pl.BlockSpec((tm, d), lambda i: (i, 0)),
        out_shape=jax.ShapeDtypeStruct((n, d), F32),
        compiler_params=_params("parallel"),
        name="out_proj",
    )(merged, w_o, g_post, x)


def _rot_cols(w):
    half = w.shape[-1] // 2
    return jnp.concatenate([-w[..., half:], w[..., :half]], axis=-1)


def _prep_ffn(wi, wo):
    pad = D_FF_PAD - D_FF
    wa = jnp.pad(wi[:, :D_FF], ((0, 0), (0, pad)))
    wb = jnp.pad(wi[:, D_FF:], ((0, 0), (0, pad)))
    return jnp.concatenate([wa, wb], axis=1).astype(BF16), jnp.pad(wo, ((0, pad), (0, 0))).astype(BF16)


def _prep_w_in(w_in):
    o_kr = Q_LORA + KV_LORA
    o_ret = o_kr + MLA_ROPE
    o_gm = o_ret + RET_GROUP_W
    w_kr = w_in[:, o_kr:o_ret]
    w_mla = jnp.concatenate([w_in[:, :o_ret], _rot_cols(w_kr)], axis=1).astype(BF16)
    return w_mla, w_in[:, o_ret:o_gm].astype(BF16), w_in[:, o_gm:].astype(BF16)


def _prep_w_uq(w_uq):
    w = w_uq.reshape(Q_LORA, MLA_HEADS, MLA_NOPE + MLA_ROPE)
    w_rope = w[..., MLA_NOPE:]
    w = jnp.concatenate([w, _rot_cols(w_rope)], axis=-1)
    return w.reshape(Q_LORA, MLA_HEADS * MLA_QK_PAD).astype(BF16)


def _row(v):
    return v.reshape(1, -1)


def _token_mixer(x, l, p, tables, b, s):
    c_r, s_r, c_k, s_k = tables
    h = _prenorm(x, _row(p["mix_pre_g"][l]))
    w_mla, w_ret, w_gm = _prep_w_in(p["w_in"][l])
    zm = _mm(h, w_mla, tn=MLA_GROUP_W)
    zr = _mm(h, w_ret, tn=1024)
    zg = _mm(h, w_gm, tn=1024)
    q, kv, kr = _mla_prep(zm, _row(p["q_norm_g"][l]), _row(p["kv_norm_g"][l]), _prep_w_uq(p["w_uq"][l]),
                          p["w_ukv"][l].astype(BF16), c_r, s_r)
    y_a = _attention(q, kv, kr, b, s)
    y_b = _retention(zr, c_k, s_k, b, s)
    y_c = _gmlp(zg, _row(p["gm_ln_g"][l]), _row(p["gm_ln_b"][l]), p["gm_w_s"][l], p["gm_b_s"][l].T)
    merged = _merge(h, y_a, y_b, y_c, p["w_gate"][l].astype(BF16), _row(p["b_gate"][l]), p["w_br"][l].astype(BF16))
    return _out_proj(merged, p["w_o"][l].astype(BF16), _row(p["mix_post_g"][l]), x)


def _trunk(x, pos, p, depth):
    b, s, d = x.shape
    tables = _rope_tables(pos)
    x = x.reshape(b * s, d)
    for l in range(depth):
        wi, wo = _prep_ffn(p["ffn1_wi"][l], p["ffn1_wo"][l])
        x = _ffn(x, _row(p["ffn1_pre_g"][l]), wi, wo, _row(p["ffn1_post_g"][l]))
        x = _token_mixer(x, l, p, tables, b, s)
        wi, wo = _prep_ffn(p["ffn2_wi"][l], p["ffn2_wo"][l])
        x = _ffn(x, _row(p["ffn2_pre_g"][l]), wi, wo, _row(p["ffn2_post_g"][l]))
    return x.reshape(b, s, d)


def kernel(x, pos, ffn1_pre_g, ffn1_wi, ffn1_wo, ffn1_post_g, mix_pre_g, w_in, q_norm_g, w_uq, kv_norm_g, w_ukv, gm_ln_g, gm_ln_b, gm_w_s, gm_b_s, w_gate, b_gate, w_br, w_o, mix_post_g, ffn2_pre_g, ffn2_wi, ffn2_wo, ffn2_post_g):
    p = dict(ffn1_pre_g=ffn1_pre_g, ffn1_wi=ffn1_wi, ffn1_wo=ffn1_wo, ffn1_post_g=ffn1_post_g, mix_pre_g=mix_pre_g,
             w_in=w_in, q_norm_g=q_norm_g, w_uq=w_uq, kv_norm_g=kv_norm_g, w_ukv=w_ukv, gm_ln_g=gm_ln_g,
             gm_ln_b=gm_ln_b, gm_w_s=gm_w_s, gm_b_s=gm_b_s, w_gate=w_gate, b_gate=b_gate, w_br=w_br, w_o=w_o,
             mix_post_g=mix_post_g, ffn2_pre_g=ffn2_pre_g, ffn2_wi=ffn2_wi, ffn2_wo=ffn2_wo, ffn2_post_g=ffn2_post_g)
    return _trunk(x, pos, p, DEPTH)
```

```python
import functools

import numpy as np
import jax
import jax.numpy as jnp
from jax import lax
from jax.experimental import pallas as pl
from jax.experimental.pallas import tpu as pltpu

F32 = jnp.float32
BF16 = jnp.bfloat16

D_MODEL = 2048
DEPTH = 4
CHUNK = 64
EPS = 1e-6
ROPE_BASE = 10000.0
MLA_HEADS = 16
MLA_NOPE = 128
MLA_ROPE = 64
MLA_V = 128
Q_LORA = 512
KV_LORA = 512
RET_HEADS = 8
RET_QK = 256
RET_V = 256
RET_GN_EPS = 1e-5
GM_GROUPS = 4
GM_WIDTH = 2048
GM_BLOCK = 128
N_BRANCH = 3
D_FF = 5504

LANE = 128
MLA_QK_PAD = 2 * LANE
MLA_GROUP_W = Q_LORA + KV_LORA + LANE
RET_GROUP_W = 4 * RET_HEADS * RET_QK
GM_GROUP_W = 2 * GM_WIDTH
FF_TILE = 512
D_FF_PAD = ((D_FF + FF_TILE - 1) // FF_TILE) * FF_TILE
VMEM_LIMIT = 56 * 2 ** 20


def _params(*sem):
    return pltpu.CompilerParams(dimension_semantics=sem, vmem_limit_bytes=VMEM_LIMIT)


def _rms(x, g):
    return x * lax.rsqrt(jnp.mean(x * x, axis=-1, keepdims=True) + EPS) * g


def _rope_table_kernel(ang_r_ref, ang_k_ref, cr_ref, sr_ref, ck_ref, sk_ref):
    a = ang_r_ref[...]
    live = lax.broadcasted_iota(jnp.int32, a.shape, 1) < MLA_ROPE
    cr_ref[...] = jnp.where(live, jnp.cos(a), 0.0)
    sr_ref[...] = jnp.where(live, jnp.sin(a), 0.0)
    k = ang_k_ref[...]
    ck_ref[...] = jnp.cos(k)
    sk_ref[...] = jnp.sin(k)


def _rope_tables(pos):
    n = pos.size
    p = pos.astype(F32).reshape(n, 1)
    inv_r = ROPE_BASE ** (-jnp.arange(0, MLA_ROPE, 2, dtype=F32) / MLA_ROPE)
    inv_k = ROPE_BASE ** (-jnp.arange(0, RET_QK, 2, dtype=F32) / RET_QK)
    ang_r = p * inv_r
    ang_r = jnp.concatenate([ang_r, ang_r, jnp.zeros((n, LANE - MLA_ROPE), F32)], axis=1)
    ang_k = p * inv_k
    tm = min(n, 1024)
    spec = pl.BlockSpec((tm, LANE), lambda i: (i, 0))
    out = jax.ShapeDtypeStruct((n, LANE), F32)
    return pl.pallas_call(
        _rope_table_kernel,
        grid=(n // tm,),
        in_specs=[spec, spec],
        out_specs=[spec] * 4,
        out_shape=[out] * 4,
        compiler_params=_params("parallel"),
        name="rope_tables",
    )(ang_r, ang_k)


def _ffn_kernel(x_ref, gpre_ref, wa_ref, wb_ref, wo_ref, gpost_ref, o_ref, h_ref, acc_ref):
    f = pl.program_id(1)

    @pl.when(f == 0)
    def _():
        h_ref[...] = _rms(x_ref[...], gpre_ref[...]).astype(BF16)
        acc_ref[...] = jnp.zeros_like(acc_ref)

    h = h_ref[...]
    a = jnp.dot(h, wa_ref[...], preferred_element_type=F32)
    b = jnp.dot(h, wb_ref[...], preferred_element_type=F32)
    act = (a * jax.nn.sigmoid(a) * b).astype(BF16)
    acc_ref[...] += jnp.dot(act, wo_ref[...], preferred_element_type=F32)

    @pl.when(f == pl.num_programs(1) - 1)
    def _():
        o_ref[...] = x_ref[...] + 0.5 * _rms(acc_ref[...], gpost_ref[...])


def _ffn(x, g_pre, wi, wo, g_post, l, tm=512):
    n, d = x.shape
    nf = D_FF_PAD // FF_TILE
    tm = min(tm, n)
    return pl.pallas_call(
        _ffn_kernel,
        grid=(n // tm, nf),
        in_specs=[
            pl.BlockSpec((tm, d), lambda i, f: (i, 0)),
            pl.BlockSpec((1, d), lambda i, f: (0, 0)),
            pl.BlockSpec((None, d, FF_TILE), lambda i, f: (l, 0, f)),
            pl.BlockSpec((None, d, FF_TILE), lambda i, f: (l, 0, f + nf)),
            pl.BlockSpec((None, FF_TILE, d), lambda i, f: (l, f, 0)),
            pl.BlockSpec((1, d), lambda i, f: (0, 0)),
        ],
        out_specs=pl.BlockSpec((tm, d), lambda i, f: (i, 0)),
        out_shape=jax.ShapeDtypeStruct((n, d), F32),
        scratch_shapes=[pltpu.VMEM((tm, d), BF16), pltpu.VMEM((tm, d), F32)],
        compiler_params=_params("parallel", "arbitrary"),
        name="ffn",
    )(x, g_pre, wi, wi, wo, g_post)


def _prenorm_kernel(x_ref, g_ref, o_ref):
    o_ref[...] = _rms(x_ref[...], g_ref[...]).astype(BF16)


def _prenorm(x, g, tm=1024):
    n, d = x.shape
    tm = min(tm, n)
    return pl.pallas_call(
        _prenorm_kernel,
        grid=(n // tm,),
        in_specs=[pl.BlockSpec((tm, d), lambda i: (i, 0)), pl.BlockSpec((1, d), lambda i: (0, 0))],
        out_specs=pl.BlockSpec((tm, d), lambda i: (i, 0)),
        out_shape=jax.ShapeDtypeStruct((n, d), BF16),
        compiler_params=_params("parallel"),
        name="prenorm",
    )(x, g)


def _mm_kernel(a_ref, w_ref, o_ref):
    o_ref[...] = jnp.dot(a_ref[...], w_ref[...], preferred_element_type=F32).astype(o_ref.dtype)


def _mm(a, w, l, tn, tm=1024):
    n, k = a.shape
    nc = w.shape[2]
    tm = min(tm, n)
    return pl.pallas_call(
        _mm_kernel,
        grid=(n // tm, nc // tn),
        in_specs=[pl.BlockSpec((tm, k), lambda i, j: (i, 0)), pl.BlockSpec((None, k, tn), lambda i, j: (l, 0, j))],
        out_specs=pl.BlockSpec((tm, tn), lambda i, j: (i, j)),
        out_shape=jax.ShapeDtypeStruct((n, nc), BF16),
        compiler_params=_params("parallel", "arbitrary"),
        name="in_proj",
    )(a, w)


def _rope_half_block(blk, c, s):
    return blk * c + pltpu.roll(blk, MLA_ROPE, axis=1) * s


def _mla_prep_kernel(z_ref, gq_ref, gkv_ref, wq_ref, wkv_ref, c_ref, s_ref, q_ref, kv_ref, kr_ref):
    c = c_ref[...]
    s = s_ref[...]
    hq = _rms(z_ref[:, :Q_LORA].astype(F32), gq_ref[...]).astype(BF16)
    for h in range(MLA_HEADS):
        lo = h * MLA_QK_PAD
        qh = jnp.dot(hq, wq_ref[:, lo:lo + MLA_QK_PAD], preferred_element_type=F32)
        q_ref[:, lo:lo + LANE] = qh[:, :LANE].astype(BF16)
        q_ref[:, lo + LANE:lo + MLA_QK_PAD] = _rope_half_block(qh[:, LANE:], c, s).astype(BF16)
    hkv = _rms(z_ref[:, Q_LORA:Q_LORA + KV_LORA].astype(F32), gkv_ref[...]).astype(BF16)
    step = 4 * (MLA_NOPE + MLA_V)
    for lo in range(0, MLA_HEADS * (MLA_NOPE + MLA_V), step):
        kv_ref[:, lo:lo + step] = jnp.dot(hkv, wkv_ref[:, lo:lo + step], preferred_element_type=F32).astype(BF16)
    kr_ref[...] = _rope_half_block(z_ref[:, Q_LORA + KV_LORA:].astype(F32), c, s).astype(BF16)


def _mla_prep(zm, gq, gkv, wq, wkv, c_r, s_r, l, tm=256):
    n = zm.shape[0]
    tm = min(tm, n)
    wq_w = MLA_HEADS * MLA_QK_PAD
    wkv_w = MLA_HEADS * (MLA_NOPE + MLA_V)
    row = lambda w: pl.BlockSpec((tm, w), lambda i: (i, 0))
    full = lambda r, w: pl.BlockSpec((r, w), lambda i: (0, 0))
    layer = lambda r, w: pl.BlockSpec((None, r, w), lambda i: (l, 0, 0))
    return pl.pallas_call(
        _mla_prep_kernel,
        grid=(n // tm,),
        in_specs=[row(MLA_GROUP_W), full(1, Q_LORA), full(1, KV_LORA), layer(Q_LORA, wq_w), layer(KV_LORA, wkv_w),
                  row(LANE), row(LANE)],
        out_specs=[row(wq_w), row(wkv_w), row(LANE)],
        out_shape=[jax.ShapeDtypeStruct((n, wq_w), BF16), jax.ShapeDtypeStruct((n, wkv_w), BF16),
                   jax.ShapeDtypeStruct((n, LANE), BF16)],
        compiler_params=_params("parallel"),
        name="mla_prep",
    )(zm, gq, gkv, wq, wkv, c_r, s_r)


def _attn_kernel(q_ref, kv_ref, kr_ref, o_ref, *, t, hp):
    i = pl.program_id(2)
    c = (MLA_NOPE + MLA_ROPE) ** -0.5 * np.log2(np.e)
    hw = MLA_NOPE + MLA_V

    def tile(k0, carry, mask):
        kr = kr_ref[pl.ds(k0, t), :]
        new = []
        for j in range(hp):
            m, l, acc = carry[j]
            q = q_ref[:, j * MLA_QK_PAD:(j + 1) * MLA_QK_PAD]
            k = jnp.concatenate([kv_ref[pl.ds(k0, t), j * hw:j * hw + MLA_NOPE], kr], axis=1)
            s = lax.dot_general(q, k, (((1,), (1,)), ((), ())), preferred_element_type=F32)
            if mask is not None:
                s = jnp.where(mask, s, -1e30)
            m_new = jnp.maximum(m, jnp.max(s, axis=-1, keepdims=True))
            alpha = jnp.exp2((m - m_new) * c)
            p = jnp.exp2((s - m_new) * c)
            l = alpha * l + jnp.sum(p, axis=-1, keepdims=True)
            v = kv_ref[pl.ds(k0, t), j * hw + MLA_NOPE:(j + 1) * hw]
            acc = alpha * acc + jnp.dot(p.astype(BF16), v, preferred_element_type=F32)
            new.append((m_new, l, acc))
        return tuple(new)

    def body(kb, carry):
        return tile(pl.multiple_of(kb * t, t), carry, None)

    init = tuple((jnp.full((t, 1), -1e30, F32), jnp.zeros((t, 1), F32), jnp.zeros((t, MLA_V), F32))
                 for _ in range(hp))
    carry = lax.fori_loop(0, i, body, init)
    qc = lax.broadcasted_iota(jnp.int32, (t, t), 0) // CHUNK
    kc = lax.broadcasted_iota(jnp.int32, (t, t), 1) // CHUNK
    carry = tile(pl.multiple_of(i * t, t), carry, kc <= qc)
    for j in range(hp):
        m, l, acc = carry[j]
        o_ref[:, j * MLA_V:(j + 1) * MLA_V] = (acc / l).astype(BF16)


def _attention(q, kv, kr, b, s, t=512, hp=2):
    n = q.shape[0]
    t = min(t, s)
    nq = s // t
    kv3 = kv.reshape(b, s, kv.shape[1])
    kr3 = kr.reshape(b, s, LANE)
    return pl.pallas_call(
        functools.partial(_attn_kernel, t=t, hp=hp),
        grid=(b, MLA_HEADS // hp, nq),
        in_specs=[
            pl.BlockSpec((t, hp * MLA_QK_PAD), lambda bi, h, i: (bi * nq + i, h)),
            pl.BlockSpec((None, s, hp * (MLA_NOPE + MLA_V)), lambda bi, h, i: (bi, 0, h)),
            pl.BlockSpec((None, s, LANE), lambda bi, h, i: (bi, 0, 0)),
        ],
        out_specs=pl.BlockSpec((t, hp * MLA_V), lambda bi, h, i: (bi * nq + i, h)),
        out_shape=jax.ShapeDtypeStruct((n, MLA_HEADS * MLA_V), BF16),
        compiler_params=_params("parallel", "parallel", "arbitrary"),
        name="mla_attention",
    )(q, kv3, kr3)


def _ret_kernel(lg_ref, q_ref, k_ref, v_ref, g_ref, c_ref, s_ref, o_ref, state_ref, dec_ref, *, t, hp):
    first = pl.program_id(2) == 0
    c = c_ref[...]
    s = s_ref[...]
    half = RET_QK // 2
    pos = lax.broadcasted_iota(jnp.int32, (t, 1), 0).astype(F32)

    def rope(x):
        x1 = x[:, :half]
        x2 = x[:, half:]
        return jnp.concatenate([x1 * c - x2 * s, x2 * c + x1 * s], axis=1)

    for j in range(hp):
        lg = lg_ref[pl.program_id(1) * hp + j]
        cols = slice(j * RET_QK, (j + 1) * RET_QK)

        @pl.when(first)
        def _():
            state_ref[j] = jnp.zeros((RET_QK, RET_V), F32)
            ii = lax.broadcasted_iota(jnp.int32, (t, t), 0)
            jj = lax.broadcasted_iota(jnp.int32, (t, t), 1)
            dec_ref[j] = jnp.where(jj // CHUNK <= ii // CHUNK, jnp.exp(jnp.abs(ii - jj).astype(F32) * lg), 0.0)

        q = rope(q_ref[:, cols].astype(F32))
        k = rope(k_ref[:, cols].astype(F32)) * RET_QK ** -0.5
        v = v_ref[:, cols]
        q_dec = q * jnp.exp((pos + 1.0) * lg)
        k_dec = k * jnp.exp((t - 1.0 - pos) * lg)
        a = lax.dot_general(q.astype(BF16), k.astype(BF16), (((1,), (1,)), ((), ())),
                            preferred_element_type=F32) * dec_ref[j]
        state = state_ref[j]
        o = jnp.dot(a.astype(BF16), v, preferred_element_type=F32)
        o = o + jnp.dot(q_dec.astype(BF16), state.astype(BF16), preferred_element_type=F32)
        block_decay = jnp.exp(jnp.full((1, RET_V), t * 1.0, F32) * lg)
        state_ref[j] = state * block_decay + lax.dot_general(
            k_dec.astype(BF16), v, (((0,), (0,)), ((), ())), preferred_element_type=F32)

        mu = jnp.mean(o, axis=-1, keepdims=True)
        d = o - mu
        var = jnp.mean(d * d, axis=-1, keepdims=True)
        on = d * lax.rsqrt(var + RET_GN_EPS)
        g = g_ref[:, cols].astype(F32)
        o_ref[:, cols] = (g * jax.nn.sigmoid(g) * on).astype(BF16)


def _retention(zr, c_k, s_k, b, s, t=256, hp=2):
    n = zr.shape[0]
    t = min(t, s)
    nt = s // t
    ng = RET_HEADS // hp
    log_g = jnp.log1p(-(2.0 ** (-5.0 - jnp.arange(RET_HEADS, dtype=F32))))
    col = lambda part: pl.BlockSpec((t, hp * RET_QK), lambda bi, h, ti: (bi * nt + ti, part * ng + h))
    tab = pl.BlockSpec((t, LANE), lambda bi, h, ti: (bi * nt + ti, 0))
    return pl.pallas_call(
        functools.partial(_ret_kernel, t=t, hp=hp),
        grid=(b, ng, nt),
        in_specs=[pl.BlockSpec(memory_space=pltpu.SMEM), col(0), col(1), col(2), col(3), tab, tab],
        out_specs=pl.BlockSpec((t, hp * RET_V), lambda bi, h, ti: (bi * nt + ti, h)),
        out_shape=jax.ShapeDtypeStruct((n, RET_HEADS * RET_V), BF16),
        scratch_shapes=[pltpu.VMEM((hp, RET_QK, RET_V), F32), pltpu.VMEM((hp, t, t), F32)],
        compiler_params=_params("parallel", "parallel", "arbitrary"),
        name="retention",
    )(log_g, zr, zr, zr, zr, c_k, s_k)


def _gmlp_kernel(u_ref, v_ref, lng_ref, lnb_ref, ws_ref, bst_ref, o_ref, *, nblk):
    v = jax.nn.gelu(v_ref[...].astype(F32))
    mu = jnp.mean(v, axis=-1, keepdims=True)
    d = v - mu
    var = jnp.mean(d * d, axis=-1, keepdims=True)
    vn = (d * lax.rsqrt(var + EPS) * lng_ref[...] + lnb_ref[...]).astype(BF16)
    pc_i = lax.broadcasted_iota(jnp.int32, (GM_BLOCK, GM_BLOCK), 0) // CHUNK
    pc_j = lax.broadcasted_iota(jnp.int32, (GM_BLOCK, GM_BLOCK), 1) // CHUNK
    gw = GM_WIDTH // GM_GROUPS
    for g in range(GM_GROUPS):
        w = jnp.where(pc_i >= pc_j, ws_ref[g], 0.0).astype(BF16)
        bias = bst_ref[:, g:g + 1]
        for r in range(nblk):
            rows = slice(r * GM_BLOCK, (r + 1) * GM_BLOCK)
            cols = slice(g * gw, (g + 1) * gw)
            mixed = jnp.dot(w, vn[rows, cols], preferred_element_type=F32) + bias
            u = jax.nn.gelu(u_ref[rows, cols].astype(F32))
            o_ref[rows, cols] = (u * mixed).astype(BF16)


def _gmlp(zg, ln_g, ln_b, w_s, b_s_t, l, nblk=2):
    n = zg.shape[0]
    tm = nblk * GM_BLOCK
    return pl.pallas_call(
        functools.partial(_gmlp_kernel, nblk=nblk),
        grid=(n // tm,),
        in_specs=[
            pl.BlockSpec((tm, GM_WIDTH), lambda i: (i, 0)),
            pl.BlockSpec((tm, GM_WIDTH), lambda i: (i, 1)),
            pl.BlockSpec((1, GM_WIDTH), lambda i: (0, 0)),
            pl.BlockSpec((1, GM_WIDTH), lambda i: (0, 0)),
            pl.BlockSpec((None, GM_GROUPS, GM_BLOCK, GM_BLOCK), lambda i: (l, 0, 0, 0)),
            pl.BlockSpec((GM_BLOCK, GM_GROUPS), lambda i: (0, 0)),
        ],
        out_specs=pl.BlockSpec((tm, GM_WIDTH), lambda i: (i, 0)),
        out_shape=jax.ShapeDtypeStruct((n, GM_WIDTH), BF16),
        compiler_params=_params("parallel"),
        name="gmlp",
    )(zg, zg, ln_g, ln_b, w_s, b_s_t)


def _merge_kernel(h_ref, ya_ref, yb_ref, yc_ref, wg0_ref, wg1_ref, wg2_ref, bg0_ref, bg1_ref, bg2_ref,
                  wb0_ref, wb1_ref, wb2_ref, o_ref):
    h = h_ref[...]

    def branch(y_ref, wg_ref, bg_ref, wb_ref):
        gate = jax.nn.sigmoid(jnp.dot(h, wg_ref[...], preferred_element_type=F32) + bg_ref[...])
        return gate * jnp.dot(y_ref[...], wb_ref[...], preferred_element_type=F32)

    merged = branch(ya_ref, wg0_ref, bg0_ref, wb0_ref)
    merged = merged + branch(yb_ref, wg1_ref, bg1_ref, wb1_ref)
    merged = merged + branch(yc_ref, wg2_ref, bg2_ref, wb2_ref)
    o_ref[...] = merged.astype(BF16)


def _merge(h, ya, yb, yc, w_gate, b_gate, w_br, l, tm=512, tn=256):
    n, d = h.shape
    tm = min(tm, n)
    nj = d // tn
    act = pl.BlockSpec((tm, d), lambda i, j: (i, 0))
    wg = lambda br: pl.BlockSpec((None, d, tn), lambda i, j: (l, 0, br * nj + j))
    bg = lambda br: pl.BlockSpec((1, tn), lambda i, j: (0, br * nj + j))
    wb = lambda br: pl.BlockSpec((None, None, d, tn), lambda i, j: (l, br, 0, j))
    return pl.pallas_call(
        _merge_kernel,
        grid=(n // tm, nj),
        in_specs=[act, act, act, act, wg(0), wg(1), wg(2), bg(0), bg(1), bg(2), wb(0), wb(1), wb(2)],
        out_specs=pl.BlockSpec((tm, tn), lambda i, j: (i, j)),
        out_shape=jax.ShapeDtypeStruct((n, d), BF16),
        compiler_params=_params("parallel", "arbitrary"),
        name="merge",
    )(h, ya, yb, yc, w_gate, w_gate, w_gate, b_gate, b_gate, b_gate, w_br, w_br, w_br)


def _out_proj_kernel(m_ref, w_ref, g_ref, x_ref, o_ref):
    y = jnp.dot(m_ref[...], w_ref[...], preferred_element_type=F32)
    o_ref[...] = x_ref[...] + _rms(y, g_ref[...])


def _out_proj(merged, w_o, g_post, x, l, tm=512):
    n, d = x.shape
    tm = min(tm, n)
    return pl.pallas_call(
        _out_proj_kernel,
        grid=(n // tm,),
        in_specs=[
            pl.BlockSpec((tm, d), lambda i: (i, 0)),
            pl.BlockSpec((None, d, d), lambda i: (l, 0, 0)),
            pl.BlockSpec((1, d), lambda i: (0, 0)),
            pl.BlockSpec((tm, d), lambda i: (i, 0)),
        ],
        out_specs=pl.BlockSpec((tm, d), lambda i: (i, 0)),
        out_shape=jax.ShapeDtypeStruct((n, d), F32),
        compiler_params=_params("parallel"),
        name="out_proj",
    )(merged, w_o, g_post, x)


def _rot_cols(w):
    half = w.shape[-1] // 2
    return jnp.concatenate([-w[..., half:], w[..., :half]], axis=-1)


def _prep_ffn(wi, wo):
    nl, d, _ = wi.shape
    pad = D_FF_PAD - D_FF
    wi = jnp.pad(wi.reshape(nl, d, 2, D_FF), ((0, 0), (0, 0), (0, 0), (0, pad))).astype(BF16)
    return wi.reshape(nl, d, 2 * D_FF_PAD), jnp.pad(wo, ((0, 0), (0, pad), (0, 0))).astype(BF16)


def _prep_w_in(w_in):
    o_kr = Q_LORA + KV_LORA
    o_ret = o_kr + MLA_ROPE
    o_gm = o_ret + RET_GROUP_W
    w_kr = w_in[..., o_kr:o_ret]
    w_mla = jnp.concatenate([w_in[..., :o_ret], _rot_cols(w_kr)], axis=-1).astype(BF16)
    return w_mla, w_in[..., o_ret:o_gm].astype(BF16), w_in[..., o_gm:].astype(BF16)


def _prep_w_uq(w_uq):
    nl = w_uq.shape[0]
    w = w_uq.reshape(nl, Q_LORA, MLA_HEADS, MLA_NOPE + MLA_ROPE)
    w_rope = w[..., MLA_NOPE:]
    w = jnp.concatenate([w, _rot_cols(w_rope)], axis=-1)
    return w.reshape(nl, Q_LORA, MLA_HEADS * MLA_QK_PAD).astype(BF16)


def _prep_weights(p):
    w = {}
    w["ffn1_wi"], w["ffn1_wo"] = _prep_ffn(p["ffn1_wi"], p["ffn1_wo"])
    w["ffn2_wi"], w["ffn2_wo"] = _prep_ffn(p["ffn2_wi"], p["ffn2_wo"])
    w["w_mla"], w["w_ret"], w["w_gm"] = _prep_w_in(p["w_in"])
    w["w_uq"] = _prep_w_uq(p["w_uq"])
    for name in ("w_ukv", "w_gate", "w_br", "w_o"):
        w[name] = p[name].astype(BF16)
    w["gm_b_s_t"] = jnp.swapaxes(p["gm_b_s"], 1, 2)
    return w


def _row(v):
    return v.reshape(1, -1)


def _token_mixer(x, l, p, w, tables, b, s):
    c_r, s_r, c_k, s_k = tables
    h = _prenorm(x, _row(p["mix_pre_g"][l]))
    zm = _mm(h, w["w_mla"], l, tn=MLA_GROUP_W)
    zr = _mm(h, w["w_ret"], l, tn=1024)
    zg = _mm(h, w["w_gm"], l, tn=1024)
    q, kv, kr = _mla_prep(zm, _row(p["q_norm_g"][l]), _row(p["kv_norm_g"][l]), w["w_uq"], w["w_ukv"], c_r, s_r, l)
    y_a = _attention(q, kv, kr, b, s)
    y_b = _retention(zr, c_k, s_k, b, s)
    y_c = _gmlp(zg, _row(p["gm_ln_g"][l]), _row(p["gm_ln_b"][l]), p["gm_w_s"], w["gm_b_s_t"][l], l)
    merged = _merge(h, y_a, y_b, y_c, w["w_gate"], _row(p["b_gate"][l]), w["w_br"], l)
    return _out_proj(merged, w["w_o"], _row(p["mix_post_g"][l]), x, l)


def _trunk(x, pos, p, depth):
    b, s, d = x.shape
    tables = _rope_tables(pos)
    w = _prep_weights(p)
    x = x.reshape(b * s, d)
    for l in range(depth):
        x = _ffn(x, _row(p["ffn1_pre_g"][l]), w["ffn1_wi"], w["ffn1_wo"], _row(p["ffn1_post_g"][l]), l)
        x = _token_mixer(x, l, p, w, tables, b, s)
        x = _ffn(x, _row(p["ffn2_pre_g"][l]), w["ffn2_wi"], w["ffn2_wo"], _row(p["ffn2_post_g"][l]), l)
    return x.reshape(b, s, d)


def kernel(x, pos, ffn1_pre_g, ffn1_wi, ffn1_wo, ffn1_post_g, mix_pre_g, w_in, q_norm_g, w_uq, kv_norm_g, w_ukv, gm_ln_g, gm_ln_b, gm_w_s, gm_b_s, w_gate, b_gate, w_br, w_o, mix_post_g, ffn2_pre_g, ffn2_wi, ffn2_wo, ffn2_post_g):
    p = dict(ffn1_pre_g=ffn1_pre_g, ffn1_wi=ffn1_wi, ffn1_wo=ffn1_wo, ffn1_post_g=ffn1_post_g, mix_pre_g=mix_pre_g,
             w_in=w_in, q_norm_g=q_norm_g, w_uq=w_uq, kv_norm_g=kv_norm_g, w_ukv=w_ukv, gm_ln_g=gm_ln_g,
             gm_ln_b=gm_ln_b, gm_w_s=gm_w_s, gm_b_s=gm_b_s, w_gate=w_gate, b_gate=b_gate, w_br=w_br, w_o=w_o,
             mix_post_g=mix_post_g, ffn2_pre_g=ffn2_pre_g, ffn2_wi=ffn2_wi, ffn2_wo=ffn2_wo, ffn2_post_g=ffn2_post_g)
    return _trunk(x, pos, p, DEPTH)
```

```python
import functools

import numpy as np
import jax
import jax.numpy as jnp
from jax import lax
from jax.experimental import pallas as pl
from jax.experimental.pallas import tpu as pltpu

F32 = jnp.float32
BF16 = jnp.bfloat16

D_MODEL = 2048
DEPTH = 4
CHUNK = 64
EPS = 1e-6
ROPE_BASE = 10000.0
MLA_HEADS = 16
MLA_NOPE = 128
MLA_ROPE = 64
MLA_V = 128
Q_LORA = 512
KV_LORA = 512
RET_HEADS = 8
RET_QK = 256
RET_V = 256
RET_GN_EPS = 1e-5
GM_GROUPS = 4
GM_WIDTH = 2048
GM_BLOCK = 128
N_BRANCH = 3
D_FF = 5504

LANE = 128
MLA_QK_PAD = 2 * LANE
MLA_GROUP_W = Q_LORA + KV_LORA + LANE
RET_GROUP_W = 4 * RET_HEADS * RET_QK
GM_GROUP_W = 2 * GM_WIDTH
FF_TILE = 512
D_FF_PAD = ((D_FF + FF_TILE - 1) // FF_TILE) * FF_TILE
VMEM_LIMIT = 56 * 2 ** 20


def _params(*sem):
    return pltpu.CompilerParams(dimension_semantics=sem, vmem_limit_bytes=VMEM_LIMIT)


def _rms(x, g):
    return x * lax.rsqrt(jnp.mean(x * x, axis=-1, keepdims=True) + EPS) * g


def _rope_table_kernel(ang_r_ref, ang_k_ref, cr_ref, sr_ref, ck_ref, sk_ref):
    a = ang_r_ref[...]
    live = lax.broadcasted_iota(jnp.int32, a.shape, 1) < MLA_ROPE
    cr_ref[...] = jnp.where(live, jnp.cos(a), 0.0)
    sr_ref[...] = jnp.where(live, jnp.sin(a), 0.0)
    k = ang_k_ref[...]
    ck_ref[...] = jnp.cos(k)
    sk_ref[...] = jnp.sin(k)


def _rope_tables(pos):
    n = pos.size
    p = pos.astype(F32).reshape(n, 1)
    inv_r = ROPE_BASE ** (-jnp.arange(0, MLA_ROPE, 2, dtype=F32) / MLA_ROPE)
    inv_k = ROPE_BASE ** (-jnp.arange(0, RET_QK, 2, dtype=F32) / RET_QK)
    ang_r = p * inv_r
    ang_r = jnp.concatenate([ang_r, ang_r, jnp.zeros((n, LANE - MLA_ROPE), F32)], axis=1)
    ang_k = p * inv_k
    tm = min(n, 1024)
    spec = pl.BlockSpec((tm, LANE), lambda i: (i, 0))
    out = jax.ShapeDtypeStruct((n, LANE), F32)
    return pl.pallas_call(
        _rope_table_kernel,
        grid=(n // tm,),
        in_specs=[spec, spec],
        out_specs=[spec] * 4,
        out_shape=[out] * 4,
        compiler_params=_params("parallel"),
        name="rope_tables",
    )(ang_r, ang_k)


def _ffn_kernel(x_ref, gpre_ref, wa_ref, wb_ref, wo_ref, gpost_ref, o_ref, h_ref, acc_ref):
    f = pl.program_id(1)

    @pl.when(f == 0)
    def _():
        h_ref[...] = _rms(x_ref[...], gpre_ref[...]).astype(BF16)
        acc_ref[...] = jnp.zeros_like(acc_ref)

    h = h_ref[...]
    a = jnp.dot(h, wa_ref[...], preferred_element_type=F32)
    b = jnp.dot(h, wb_ref[...], preferred_element_type=F32)
    act = (a * jax.nn.sigmoid(a) * b).astype(BF16)
    acc_ref[...] += jnp.dot(act, wo_ref[...], preferred_element_type=F32)

    @pl.when(f == pl.num_programs(1) - 1)
    def _():
        o_ref[...] = x_ref[...] + 0.5 * _rms(acc_ref[...], gpost_ref[...])


def _ffn(x, g_pre, wa, wb, wo, g_post, l, tm=512):
    n, d = x.shape
    nf = D_FF_PAD // FF_TILE
    tm = min(tm, n)
    return pl.pallas_call(
        _ffn_kernel,
        grid=(n // tm, nf),
        in_specs=[
            pl.BlockSpec((tm, d), lambda i, f: (i, 0)),
            pl.BlockSpec((1, d), lambda i, f: (0, 0)),
            pl.BlockSpec((None, d, FF_TILE), lambda i, f: (l, 0, f)),
            pl.BlockSpec((None, d, FF_TILE), lambda i, f: (l, 0, f)),
            pl.BlockSpec((None, FF_TILE, d), lambda i, f: (l, f, 0)),
            pl.BlockSpec((1, d), lambda i, f: (0, 0)),
        ],
        out_specs=pl.BlockSpec((tm, d), lambda i, f: (i, 0)),
        out_shape=jax.ShapeDtypeStruct((n, d), F32),
        scratch_shapes=[pltpu.VMEM((tm, d), BF16), pltpu.VMEM((tm, d), F32)],
        compiler_params=_params("parallel", "arbitrary"),
        name="ffn",
    )(x, g_pre, wa, wb, wo, g_post)


def _prenorm_kernel(x_ref, g_ref, o_ref):
    o_ref[...] = _rms(x_ref[...], g_ref[...]).astype(BF16)


def _prenorm(x, g, tm=1024):
    n, d = x.shape
    tm = min(tm, n)
    return pl.pallas_call(
        _prenorm_kernel,
        grid=(n // tm,),
        in_specs=[pl.BlockSpec((tm, d), lambda i: (i, 0)), pl.BlockSpec((1, d), lambda i: (0, 0))],
        out_specs=pl.BlockSpec((tm, d), lambda i: (i, 0)),
        out_shape=jax.ShapeDtypeStruct((n, d), BF16),
        compiler_params=_params("parallel"),
        name="prenorm",
    )(x, g)


def _mm_kernel(a_ref, w_ref, o_ref):
    o_ref[...] = jnp.dot(a_ref[...], w_ref[...], preferred_element_type=F32).astype(o_ref.dtype)


def _mm(a, w, l, tn, tm=1024):
    n, k = a.shape
    nc = w.shape[2]
    tm = min(tm, n)
    return pl.pallas_call(
        _mm_kernel,
        grid=(n // tm, nc // tn),
        in_specs=[pl.BlockSpec((tm, k), lambda i, j: (i, 0)), pl.BlockSpec((None, k, tn), lambda i, j: (l, 0, j))],
        out_specs=pl.BlockSpec((tm, tn), lambda i, j: (i, j)),
        out_shape=jax.ShapeDtypeStruct((n, nc), BF16),
        compiler_params=_params("parallel", "arbitrary"),
        name="in_proj",
    )(a, w)


def _rope_half_block(blk, c, s):
    return blk * c + pltpu.roll(blk, MLA_ROPE, axis=1) * s


def _mla_prep_kernel(z_ref, gq_ref, gkv_ref, wq_ref, wkv_ref, c_ref, s_ref, q_ref, kv_ref, kr_ref):
    c = c_ref[...]
    s = s_ref[...]
    hq = _rms(z_ref[:, :Q_LORA].astype(F32), gq_ref[...]).astype(BF16)
    for h in range(MLA_HEADS):
        lo = h * MLA_QK_PAD
        qh = jnp.dot(hq, wq_ref[:, lo:lo + MLA_QK_PAD], preferred_element_type=F32)
        q_ref[:, lo:lo + LANE] = qh[:, :LANE].astype(BF16)
        q_ref[:, lo + LANE:lo + MLA_QK_PAD] = _rope_half_block(qh[:, LANE:], c, s).astype(BF16)
    hkv = _rms(z_ref[:, Q_LORA:Q_LORA + KV_LORA].astype(F32), gkv_ref[...]).astype(BF16)
    step = 4 * (MLA_NOPE + MLA_V)
    for lo in range(0, MLA_HEADS * (MLA_NOPE + MLA_V), step):
        kv_ref[:, lo:lo + step] = jnp.dot(hkv, wkv_ref[:, lo:lo + step], preferred_element_type=F32).astype(BF16)
    kr_ref[...] = _rope_half_block(z_ref[:, Q_LORA + KV_LORA:].astype(F32), c, s).astype(BF16)


def _mla_prep(zm, gq, gkv, wq, wkv, c_r, s_r, l, tm=256):
    n = zm.shape[0]
    tm = min(tm, n)
    wq_w = MLA_HEADS * MLA_QK_PAD
    wkv_w = MLA_HEADS * (MLA_NOPE + MLA_V)
    row = lambda w: pl.BlockSpec((tm, w), lambda i: (i, 0))
    full = lambda r, w: pl.BlockSpec((r, w), lambda i: (0, 0))
    layer = lambda r, w: pl.BlockSpec((None, r, w), lambda i: (l, 0, 0))
    return pl.pallas_call(
        _mla_prep_kernel,
        grid=(n // tm,),
        in_specs=[row(MLA_GROUP_W), full(1, Q_LORA), full(1, KV_LORA), layer(Q_LORA, wq_w), layer(KV_LORA, wkv_w),
                  row(LANE), row(LANE)],
        out_specs=[row(wq_w), row(wkv_w), row(LANE)],
        out_shape=[jax.ShapeDtypeStruct((n, wq_w), BF16), jax.ShapeDtypeStruct((n, wkv_w), BF16),
                   jax.ShapeDtypeStruct((n, LANE), BF16)],
        compiler_params=_params("parallel"),
        name="mla_prep",
    )(zm, gq, gkv, wq, wkv, c_r, s_r)


def _attn_kernel(q_ref, kv_ref, kr_ref, o_ref, *, t, hp):
    i = pl.program_id(2)
    c = (MLA_NOPE + MLA_ROPE) ** -0.5 * np.log2(np.e)
    hw = MLA_NOPE + MLA_V

    def tile(k0, carry, mask):
        kr = kr_ref[pl.ds(k0, t), :]
        new = []
        for j in range(hp):
            m, l, acc = carry[j]
            q = q_ref[:, j * MLA_QK_PAD:(j + 1) * MLA_QK_PAD]
            k = jnp.concatenate([kv_ref[pl.ds(k0, t), j * hw:j * hw + MLA_NOPE], kr], axis=1)
            s = lax.dot_general(q, k, (((1,), (1,)), ((), ())), preferred_element_type=F32)
            if mask is not None:
                s = jnp.where(mask, s, -1e30)
            m_new = jnp.maximum(m, jnp.max(s, axis=-1, keepdims=True))
            alpha = jnp.exp2((m - m_new) * c)
            p = jnp.exp2((s - m_new) * c)
            l = alpha * l + jnp.sum(p, axis=-1, keepdims=True)
            v = kv_ref[pl.ds(k0, t), j * hw + MLA_NOPE:(j + 1) * hw]
            acc = alpha * acc + jnp.dot(p.astype(BF16), v, preferred_element_type=F32)
            new.append((m_new, l, acc))
        return tuple(new)

    def body(kb, carry):
        return tile(pl.multiple_of(kb * t, t), carry, None)

    init = tuple((jnp.full((t, 1), -1e30, F32), jnp.zeros((t, 1), F32), jnp.zeros((t, MLA_V), F32))
                 for _ in range(hp))
    carry = lax.fori_loop(0, i, body, init)
    qc = lax.broadcasted_iota(jnp.int32, (t, t), 0) // CHUNK
    kc = lax.broadcasted_iota(jnp.int32, (t, t), 1) // CHUNK
    carry = tile(pl.multiple_of(i * t, t), carry, kc <= qc)
    for j in range(hp):
        m, l, acc = carry[j]
        o_ref[:, j * MLA_V:(j + 1) * MLA_V] = (acc / l).astype(BF16)


def _attention(q, kv, kr, b, s, t=512, hp=2):
    n = q.shape[0]
    t = min(t, s)
    nq = s // t
    kv3 = kv.reshape(b, s, kv.shape[1])
    kr3 = kr.reshape(b, s, LANE)
    return pl.pallas_call(
        functools.partial(_attn_kernel, t=t, hp=hp),
        grid=(b, MLA_HEADS // hp, nq),
        in_specs=[
            pl.BlockSpec((t, hp * MLA_QK_PAD), lambda bi, h, i: (bi * nq + i, h)),
            pl.BlockSpec((None, s, hp * (MLA_NOPE + MLA_V)), lambda bi, h, i: (bi, 0, h)),
            pl.BlockSpec((None, s, LANE), lambda bi, h, i: (bi, 0, 0)),
        ],
        out_specs=pl.BlockSpec((t, hp * MLA_V), lambda bi, h, i: (bi * nq + i, h)),
        out_shape=jax.ShapeDtypeStruct((n, MLA_HEADS * MLA_V), BF16),
        compiler_params=_params("parallel", "parallel", "arbitrary"),
        name="mla_attention",
    )(q, kv3, kr3)


def _ret_kernel(lg_ref, q_ref, k_ref, v_ref, g_ref, c_ref, s_ref, o_ref, state_ref, dec_ref, *, t, hp):
    first = pl.program_id(2) == 0
    c = c_ref[...]
    s = s_ref[...]
    half = RET_QK // 2
    pos = lax.broadcasted_iota(jnp.int32, (t, 1), 0).astype(F32)

    def rope(x):
        x1 = x[:, :half]
        x2 = x[:, half:]
        return jnp.concatenate([x1 * c - x2 * s, x2 * c + x1 * s], axis=1)

    for j in range(hp):
        lg = lg_ref[pl.program_id(1) * hp + j]
        cols = slice(j * RET_QK, (j + 1) * RET_QK)

        @pl.when(first)
        def _():
            state_ref[j] = jnp.zeros((RET_QK, RET_V), F32)
            ii = lax.broadcasted_iota(jnp.int32, (t, t), 0)
            jj = lax.broadcasted_iota(jnp.int32, (t, t), 1)
            dec_ref[j] = jnp.where(jj // CHUNK <= ii // CHUNK, jnp.exp(jnp.abs(ii - jj).astype(F32) * lg), 0.0)

        q = rope(q_ref[:, cols].astype(F32))
        k = rope(k_ref[:, cols].astype(F32)) * RET_QK ** -0.5
        v = v_ref[:, cols]
        q_dec = q * jnp.exp((pos + 1.0) * lg)
        k_dec = k * jnp.exp((t - 1.0 - pos) * lg)
        a = lax.dot_general(q.astype(BF16), k.astype(BF16), (((1,), (1,)), ((), ())),
                            preferred_element_type=F32) * dec_ref[j]
        state = state_ref[j]
        o = jnp.dot(a.astype(BF16), v, preferred_element_type=F32)
        o = o + jnp.dot(q_dec.astype(BF16), state.astype(BF16), preferred_element_type=F32)
        block_decay = jnp.exp(jnp.full((1, RET_V), t * 1.0, F32) * lg)
        state_ref[j] = state * block_decay + lax.dot_general(
            k_dec.astype(BF16), v, (((0,), (0,)), ((), ())), preferred_element_type=F32)

        mu = jnp.mean(o, axis=-1, keepdims=True)
        d = o - mu
        var = jnp.mean(d * d, axis=-1, keepdims=True)
        on = d * lax.rsqrt(var + RET_GN_EPS)
        g = g_ref[:, cols].astype(F32)
        o_ref[:, cols] = (g * jax.nn.sigmoid(g) * on).astype(BF16)


def _retention(zr, c_k, s_k, b, s, t=256, hp=2):
    n = zr.shape[0]
    t = min(t, s)
    nt = s // t
    ng = RET_HEADS // hp
    log_g = jnp.log1p(-(2.0 ** (-5.0 - jnp.arange(RET_HEADS, dtype=F32))))
    col = lambda part: pl.BlockSpec((t, hp * RET_QK), lambda bi, h, ti: (bi * nt + ti, part * ng + h))
    tab = pl.BlockSpec((t, LANE), lambda bi, h, ti: (bi * nt + ti, 0))
    return pl.pallas_call(
        functools.partial(_ret_kernel, t=t, hp=hp),
        grid=(b, ng, nt),
        in_specs=[pl.BlockSpec(memory_space=pltpu.SMEM), col(0), col(1), col(2), col(3), tab, tab],
        out_specs=pl.BlockSpec((t, hp * RET_V), lambda bi, h, ti: (bi * nt + ti, h)),
        out_shape=jax.ShapeDtypeStruct((n, RET_HEADS * RET_V), BF16),
        scratch_shapes=[pltpu.VMEM((hp, RET_QK, RET_V), F32), pltpu.VMEM((hp, t, t), F32)],
        compiler_params=_params("parallel", "parallel", "arbitrary"),
        name="retention",
    )(log_g, zr, zr, zr, zr, c_k, s_k)


def _gmlp_kernel(u_ref, v_ref, lng_ref, lnb_ref, ws_ref, bst_ref, o_ref, *, nblk):
    v = jax.nn.gelu(v_ref[...].astype(F32))
    mu = jnp.mean(v, axis=-1, keepdims=True)
    d = v - mu
    var = jnp.mean(d * d, axis=-1, keepdims=True)
    vn = (d * lax.rsqrt(var + EPS) * lng_ref[...] + lnb_ref[...]).astype(BF16)
    pc_i = lax.broadcasted_iota(jnp.int32, (GM_BLOCK, GM_BLOCK), 0) // CHUNK
    pc_j = lax.broadcasted_iota(jnp.int32, (GM_BLOCK, GM_BLOCK), 1) // CHUNK
    gw = GM_WIDTH // GM_GROUPS
    for g in range(GM_GROUPS):
        w = jnp.where(pc_i >= pc_j, ws_ref[g], 0.0).astype(BF16)
        bias = bst_ref[:, g:g + 1]
        for r in range(nblk):
            rows = slice(r * GM_BLOCK, (r + 1) * GM_BLOCK)
            cols = slice(g * gw, (g + 1) * gw)
            mixed = jnp.dot(w, vn[rows, cols], preferred_element_type=F32) + bias
            u = jax.nn.gelu(u_ref[rows, cols].astype(F32))
            o_ref[rows, cols] = (u * mixed).astype(BF16)


def _gmlp(zg, ln_g, ln_b, w_s, b_s_t, l, nblk=2):
    n = zg.shape[0]
    tm = nblk * GM_BLOCK
    return pl.pallas_call(
        functools.partial(_gmlp_kernel, nblk=nblk),
        grid=(n // tm,),
        in_specs=[
            pl.BlockSpec((tm, GM_WIDTH), lambda i: (i, 0)),
            pl.BlockSpec((tm, GM_WIDTH), lambda i: (i, 1)),
            pl.BlockSpec((1, GM_WIDTH), lambda i: (0, 0)),
            pl.BlockSpec((1, GM_WIDTH), lambda i: (0, 0)),
            pl.BlockSpec((None, GM_GROUPS, GM_BLOCK, GM_BLOCK), lambda i: (l, 0, 0, 0)),
            pl.BlockSpec((GM_BLOCK, GM_GROUPS), lambda i: (0, 0)),
        ],
        out_specs=pl.BlockSpec((tm, GM_WIDTH), lambda i: (i, 0)),
        out_shape=jax.ShapeDtypeStruct((n, GM_WIDTH), BF16),
        compiler_params=_params("parallel"),
        name="gmlp",
    )(zg, zg, ln_g, ln_b, w_s, b_s_t)


def _merge_kernel(h_ref, ya_ref, yb_ref, yc_ref, wg0_ref, wg1_ref, wg2_ref, bg0_ref, bg1_ref, bg2_ref,
                  wb0_ref, wb1_ref, wb2_ref, o_ref):
    h = h_ref[...]

    def branch(y_ref, wg_ref, bg_ref, wb_ref):
        gate = jax.nn.sigmoid(jnp.dot(h, wg_ref[...], preferred_element_type=F32) + bg_ref[...])
        return gate * jnp.dot(y_ref[...], wb_ref[...], preferred_element_type=F32)

    merged = branch(ya_ref, wg0_ref, bg0_ref, wb0_ref)
    merged = merged + branch(yb_ref, wg1_ref, bg1_ref, wb1_ref)
    merged = merged + branch(yc_ref, wg2_ref, bg2_ref, wb2_ref)
    o_ref[...] = merged.astype(BF16)


def _merge(h, ya, yb, yc, w_gate, b_gate, w_br, l, tm=512, tn=256):
    n, d = h.shape
    tm = min(tm, n)
    nj = d // tn
    act = pl.BlockSpec((tm, d), lambda i, j: (i, 0))
    wg = lambda br: pl.BlockSpec((None, d, tn), lambda i, j: (l, 0, br * nj + j))
    bg = lambda br: pl.BlockSpec((1, tn), lambda i, j: (0, br * nj + j))
    wb = lambda br: pl.BlockSpec((None, None, d, tn), lambda i, j: (l, br, 0, j))
    return pl.pallas_call(
        _merge_kernel,
        grid=(n // tm, nj),
        in_specs=[act, act, act, act, wg(0), wg(1), wg(2), bg(0), bg(1), bg(2), wb(0), wb(1), wb(2)],
        out_specs=pl.BlockSpec((tm, tn), lambda i, j: (i, j)),
        out_shape=jax.ShapeDtypeStruct((n, d), BF16),
        compiler_params=_params("parallel", "arbitrary"),
        name="merge",
    )(h, ya, yb, yc, w_gate, w_gate, w_gate, b_gate, b_gate, b_gate, w_br, w_br, w_br)


def _out_proj_kernel(m_ref, w_ref, g_ref, x_ref, o_ref):
    y = jnp.dot(m_ref[...], w_ref[...], preferred_element_type=F32)
    o_ref[...] = x_ref[...] + _rms(y, g_ref[...])


def _out_proj(merged, w_o, g_post, x, l, tm=512):
    n, d = x.shape
    tm = min(tm, n)
    return pl.pallas_call(
        _out_proj_kernel,
        grid=(n // tm,),
        in_specs=[
            pl.BlockSpec((tm, d), lambda i: (i, 0)),
            pl.BlockSpec((None, d, d), lambda i: (l, 0, 0)),
            pl.BlockSpec((1, d), lambda i: (0, 0)),
            pl.BlockSpec((tm, d), lambda i: (i, 0)),
        ],
        out_specs=pl.BlockSpec((tm, d), lambda i: (i, 0)),
        out_shape=jax.ShapeDtypeStruct((n, d), F32),
        compiler_params=_params("parallel"),
        name="out_proj",
    )(merged, w_o, g_post, x)


def _rot_cols(w):
    half = w.shape[-1] // 2
    return jnp.concatenate([-w[..., half:], w[..., :half]], axis=-1)


def _prep_ffn(wi, wo):
    pad = D_FF_PAD - D_FF
    cols = lambda w: jnp.pad(w.astype(BF16), ((0, 0), (0, 0), (0, pad)))
    return cols(wi[..., :D_FF]), cols(wi[..., D_FF:]), jnp.pad(wo.astype(BF16), ((0, 0), (0, pad), (0, 0)))


def _prep_w_in(w_in):
    o_kr = Q_LORA + KV_LORA
    o_ret = o_kr + MLA_ROPE
    o_gm = o_ret + RET_GROUP_W
    w_kr = w_in[..., o_kr:o_ret]
    w_mla = jnp.concatenate([w_in[..., :o_ret], _rot_cols(w_kr)], axis=-1).astype(BF16)
    return w_mla, w_in[..., o_ret:o_gm].astype(BF16), w_in[..., o_gm:].astype(BF16)


def _prep_w_uq(w_uq):
    nl = w_uq.shape[0]
    w = w_uq.reshape(nl, Q_LORA, MLA_HEADS, MLA_NOPE + MLA_ROPE)
    w_rope = w[..., MLA_NOPE:]
    w = jnp.concatenate([w, _rot_cols(w_rope)], axis=-1)
    return w.reshape(nl, Q_LORA, MLA_HEADS * MLA_QK_PAD).astype(BF16)


def _prep_weights(p):
    w = {}
    w["ffn1"] = _prep_ffn(p["ffn1_wi"], p["ffn1_wo"])
    w["ffn2"] = _prep_ffn(p["ffn2_wi"], p["ffn2_wo"])
    w["w_mla"], w["w_ret"], w["w_gm"] = _prep_w_in(p["w_in"])
    w["w_uq"] = _prep_w_uq(p["w_uq"])
    for name in ("w_ukv", "w_gate", "w_br", "w_o"):
        w[name] = p[name].astype(BF16)
    w["gm_b_s_t"] = jnp.swapaxes(p["gm_b_s"], 1, 2)
    return w


def _row(v):
    return v.reshape(1, -1)


def _token_mixer(x, l, p, w, tables, b, s):
    c_r, s_r, c_k, s_k = tables
    h = _prenorm(x, _row(p["mix_pre_g"][l]))
    zm = _mm(h, w["w_mla"], l, tn=MLA_GROUP_W)
    zr = _mm(h, w["w_ret"], l, tn=1024)
    zg = _mm(h, w["w_gm"], l, tn=1024)
    q, kv, kr = _mla_prep(zm, _row(p["q_norm_g"][l]), _row(p["kv_norm_g"][l]), w["w_uq"], w["w_ukv"], c_r, s_r, l)
    y_a = _attention(q, kv, kr, b, s)
    y_b = _retention(zr, c_k, s_k, b, s)
    y_c = _gmlp(zg, _row(p["gm_ln_g"][l]), _row(p["gm_ln_b"][l]), p["gm_w_s"], w["gm_b_s_t"][l], l)
    merged = _merge(h, y_a, y_b, y_c, w["w_gate"], _row(p["b_gate"][l]), w["w_br"], l)
    return _out_proj(merged, w["w_o"], _row(p["mix_post_g"][l]), x, l)


def _trunk(x, pos, p, depth):
    b, s, d = x.shape
    tables = _rope_tables(pos)
    w = _prep_weights(p)
    x = x.reshape(b * s, d)
    for l in range(depth):
        x = _ffn(x, _row(p["ffn1_pre_g"][l]), *w["ffn1"], _row(p["ffn1_post_g"][l]), l)
        x = _token_mixer(x, l, p, w, tables, b, s)
        x = _ffn(x, _row(p["ffn2_pre_g"][l]), *w["ffn2"], _row(p["ffn2_post_g"][l]), l)
    return x.reshape(b, s, d)


def kernel(x, pos, ffn1_pre_g, ffn1_wi, ffn1_wo, ffn1_post_g, mix_pre_g, w_in, q_norm_g, w_uq, kv_norm_g, w_ukv, gm_ln_g, gm_ln_b, gm_w_s, gm_b_s, w_gate, b_gate, w_br, w_o, mix_post_g, ffn2_pre_g, ffn2_wi, ffn2_wo, ffn2_post_g):
    p = dict(ffn1_pre_g=ffn1_pre_g, ffn1_wi=ffn1_wi, ffn1_wo=ffn1_wo, ffn1_post_g=ffn1_post_g, mix_pre_g=mix_pre_g,
             w_in=w_in, q_norm_g=q_norm_g, w_uq=w_uq, kv_norm_g=kv_norm_g, w_ukv=w_ukv, gm_ln_g=gm_ln_g,
             gm_ln_b=gm_ln_b, gm_w_s=gm_w_s, gm_b_s=gm_b_s, w_gate=w_gate, b_gate=b_gate, w_br=w_br, w_o=w_o,
             mix_post_g=mix_post_g, ffn2_pre_g=ffn2_pre_g, ffn2_wi=ffn2_wi, ffn2_wo=ffn2_wo, ffn2_post_g=ffn2_post_g)
    return _trunk(x, pos, p, DEPTH)
```

```python
import functools

import numpy as np
import jax
import jax.numpy as jnp
from jax import lax
from jax.experimental import pallas as pl
from jax.experimental.pallas import tpu as pltpu

F32 = jnp.float32
BF16 = jnp.bfloat16

D_MODEL = 2048
DEPTH = 4
CHUNK = 64
EPS = 1e-6
ROPE_BASE = 10000.0
MLA_HEADS = 16
MLA_NOPE = 128
MLA_ROPE = 64
MLA_V = 128
Q_LORA = 512
KV_LORA = 512
RET_HEADS = 8
RET_QK = 256
RET_V = 256
RET_GN_EPS = 1e-5
GM_GROUPS = 4
GM_WIDTH = 2048
GM_BLOCK = 128
N_BRANCH = 3
D_FF = 5504

LANE = 128
MLA_QK_PAD = 2 * LANE
MLA_GROUP_W = Q_LORA + KV_LORA + LANE
RET_GROUP_W = 4 * RET_HEADS * RET_QK
GM_GROUP_W = 2 * GM_WIDTH
FF_TILE = 512
D_FF_PAD = ((D_FF + FF_TILE - 1) // FF_TILE) * FF_TILE
VMEM_LIMIT = 56 * 2 ** 20


def _params(*sem, flags=None):
    return pltpu.CompilerParams(dimension_semantics=sem, vmem_limit_bytes=VMEM_LIMIT, flags=flags)


def _rms(x, g):
    return x * lax.rsqrt(jnp.mean(x * x, axis=-1, keepdims=True) + EPS) * g


def _rope_table_kernel(ang_r_ref, ang_k_ref, cr_ref, sr_ref, ck_ref, sk_ref):
    a = ang_r_ref[...]
    live = lax.broadcasted_iota(jnp.int32, a.shape, 1) < MLA_ROPE
    cr_ref[...] = jnp.where(live, jnp.cos(a), 0.0)
    sr_ref[...] = jnp.where(live, jnp.sin(a), 0.0)
    k = ang_k_ref[...]
    ck_ref[...] = jnp.cos(k)
    sk_ref[...] = jnp.sin(k)


def _rope_tables(pos):
    n = pos.size
    p = pos.astype(F32).reshape(n, 1)
    inv_r = ROPE_BASE ** (-jnp.arange(0, MLA_ROPE, 2, dtype=F32) / MLA_ROPE)
    inv_k = ROPE_BASE ** (-jnp.arange(0, RET_QK, 2, dtype=F32) / RET_QK)
    ang_r = p * inv_r
    ang_r = jnp.concatenate([ang_r, ang_r, jnp.zeros((n, LANE - MLA_ROPE), F32)], axis=1)
    ang_k = p * inv_k
    tm = min(n, 1024)
    spec = pl.BlockSpec((tm, LANE), lambda i: (i, 0))
    out = jax.ShapeDtypeStruct((n, LANE), F32)
    return pl.pallas_call(
        _rope_table_kernel,
        grid=(n // tm,),
        in_specs=[spec, spec],
        out_specs=[spec] * 4,
        out_shape=[out] * 4,
        compiler_params=_params("parallel"),
        name="rope_tables",
    )(ang_r, ang_k)


def _ffn_kernel(x_ref, gpre_ref, wa_ref, wb_ref, wo_ref, gpost_ref, o_ref, h_ref, acc_ref):
    f = pl.program_id(1)

    def hidden_tile(h):
        a = jnp.dot(h, wa_ref[...], preferred_element_type=F32)
        b = jnp.dot(h, wb_ref[...], preferred_element_type=F32)
        act = (a * jax.nn.sigmoid(a) * b).astype(BF16)
        return jnp.dot(act, wo_ref[...], preferred_element_type=F32)

    @pl.when(f == 0)
    def _():
        h = _rms(x_ref[...], gpre_ref[...]).astype(BF16)
        h_ref[...] = h
        acc_ref[...] = hidden_tile(h)

    @pl.when(f > 0)
    def _():
        acc_ref[...] += hidden_tile(h_ref[...])

    @pl.when(f == pl.num_programs(1) - 1)
    def _():
        o_ref[...] = x_ref[...] + 0.5 * _rms(acc_ref[...], gpost_ref[...])


def _ffn(x, g_pre, wi, wo, g_post, l, tm=512):
    n, d = x.shape
    nf = D_FF_PAD // FF_TILE
    tm = min(tm, n)
    return pl.pallas_call(
        _ffn_kernel,
        grid=(n // tm, nf),
        in_specs=[
            pl.BlockSpec((tm, d), lambda i, f: (i, 0)),
            pl.BlockSpec((1, d), lambda i, f: (0, 0)),
            pl.BlockSpec((None, None, d, FF_TILE), lambda i, f: (l, 0, 0, f)),
            pl.BlockSpec((None, None, d, FF_TILE), lambda i, f: (l, 1, 0, f)),
            pl.BlockSpec((None, FF_TILE, d), lambda i, f: (l, f, 0)),
            pl.BlockSpec((1, d), lambda i, f: (0, 0)),
        ],
        out_specs=pl.BlockSpec((tm, d), lambda i, f: (i, 0)),
        out_shape=jax.ShapeDtypeStruct((n, d), F32),
        scratch_shapes=[pltpu.VMEM((tm, d), BF16), pltpu.VMEM((tm, d), F32)],
        compiler_params=_params("parallel", "arbitrary"),
        name="ffn",
    )(x, g_pre, wi, wi, wo, g_post)


def _norm_mm_kernel(x_ref, g_ref, w_ref, o_ref, h_ref):
    h = _rms(x_ref[...], g_ref[...]).astype(BF16)
    h_ref[...] = h
    o_ref[...] = jnp.dot(h, w_ref[...], preferred_element_type=F32).astype(o_ref.dtype)


def _norm_mm(x, g, w, l, tm=1024):
    n, k = x.shape
    nc = w.shape[2]
    tm = min(tm, n)
    return pl.pallas_call(
        _norm_mm_kernel,
        grid=(n // tm,),
        in_specs=[pl.BlockSpec((tm, k), lambda i: (i, 0)), pl.BlockSpec((1, k), lambda i: (0, 0)),
                  pl.BlockSpec((None, k, nc), lambda i: (l, 0, 0))],
        out_specs=[pl.BlockSpec((tm, nc), lambda i: (i, 0)), pl.BlockSpec((tm, k), lambda i: (i, 0))],
        out_shape=[jax.ShapeDtypeStruct((n, nc), BF16), jax.ShapeDtypeStruct((n, k), BF16)],
        compiler_params=_params("parallel"),
        name="norm_in_proj",
    )(x, g, w)


def _mm_kernel(a_ref, w_ref, o_ref):
    o_ref[...] = jnp.dot(a_ref[...], w_ref[...], preferred_element_type=F32).astype(o_ref.dtype)


def _mm(a, w, l, tn, tm=2048):
    n, k = a.shape
    nc = w.shape[2]
    tm = min(tm, n)
    return pl.pallas_call(
        _mm_kernel,
        grid=(n // tm, nc // tn),
        in_specs=[pl.BlockSpec((tm, k), lambda i, j: (i, 0)), pl.BlockSpec((None, k, tn), lambda i, j: (l, 0, j))],
        out_specs=pl.BlockSpec((tm, tn), lambda i, j: (i, j)),
        out_shape=jax.ShapeDtypeStruct((n, nc), BF16),
        compiler_params=_params("parallel", "arbitrary"),
        name="in_proj",
    )(a, w)


def _rope_half_block(blk, c, s):
    return blk * c + pltpu.roll(blk, MLA_ROPE, axis=1) * s


def _mla_prep_kernel(z_ref, gq_ref, gkv_ref, wq_ref, wkv_ref, c_ref, s_ref, q_ref, kv_ref, kr_ref):
    c = c_ref[...]
    s = s_ref[...]
    hq = _rms(z_ref[:, :Q_LORA].astype(F32), gq_ref[...]).astype(BF16)
    for h in range(MLA_HEADS):
        lo = h * MLA_QK_PAD
        qh = jnp.dot(hq, wq_ref[:, lo:lo + MLA_QK_PAD], preferred_element_type=F32)
        q_ref[:, lo:lo + LANE] = qh[:, :LANE].astype(BF16)
        q_ref[:, lo + LANE:lo + MLA_QK_PAD] = _rope_half_block(qh[:, LANE:], c, s).astype(BF16)
    hkv = _rms(z_ref[:, Q_LORA:Q_LORA + KV_LORA].astype(F32), gkv_ref[...]).astype(BF16)
    step = 4 * (MLA_NOPE + MLA_V)
    for lo in range(0, MLA_HEADS * (MLA_NOPE + MLA_V), step):
        kv_ref[:, lo:lo + step] = jnp.dot(hkv, wkv_ref[:, lo:lo + step], preferred_element_type=F32).astype(BF16)
    kr_ref[...] = _rope_half_block(z_ref[:, Q_LORA + KV_LORA:].astype(F32), c, s).astype(BF16)


def _mla_prep(zm, gq, gkv, wq, wkv, c_r, s_r, l, tm=256):
    n = zm.shape[0]
    tm = min(tm, n)
    wq_w = MLA_HEADS * MLA_QK_PAD
    wkv_w = MLA_HEADS * (MLA_NOPE + MLA_V)
    row = lambda w: pl.BlockSpec((tm, w), lambda i: (i, 0))
    full = lambda r, w: pl.BlockSpec((r, w), lambda i: (0, 0))
    layer = lambda r, w: pl.BlockSpec((None, r, w), lambda i: (l, 0, 0))
    return pl.pallas_call(
        _mla_prep_kernel,
        grid=(n // tm,),
        in_specs=[row(MLA_GROUP_W), full(1, Q_LORA), full(1, KV_LORA), layer(Q_LORA, wq_w), layer(KV_LORA, wkv_w),
                  row(LANE), row(LANE)],
        out_specs=[row(wq_w), row(wkv_w), row(LANE)],
        out_shape=[jax.ShapeDtypeStruct((n, wq_w), BF16), jax.ShapeDtypeStruct((n, wkv_w), BF16),
                   jax.ShapeDtypeStruct((n, LANE), BF16)],
        compiler_params=_params("parallel"),
        name="mla_prep",
    )(zm, gq, gkv, wq, wkv, c_r, s_r)


def _attn_kernel(q_ref, kv_ref, kr_ref, o_ref, *, tq, tk, hp):
    i = pl.program_id(2)
    c = (MLA_NOPE + MLA_ROPE) ** -0.5 * np.log2(np.e)
    hw = MLA_NOPE + MLA_V

    def tile(k0, carry, masked):
        kr = kr_ref[pl.ds(k0, tk), :]
        if masked:
            qc = (i * tq + lax.broadcasted_iota(jnp.int32, (tq, tk), 0)) // CHUNK
            kc = (k0 + lax.broadcasted_iota(jnp.int32, (tq, tk), 1)) // CHUNK
            mask = kc <= qc
        new = []
        for j in range(hp):
            m, l, acc = carry[j]
            q = q_ref[:, j * MLA_QK_PAD:(j + 1) * MLA_QK_PAD]
            k = jnp.concatenate([kv_ref[pl.ds(k0, tk), j * hw:j * hw + MLA_NOPE], kr], axis=1)
            s = lax.dot_general(q, k, (((1,), (1,)), ((), ())), preferred_element_type=F32)
            if masked:
                s = jnp.where(mask, s, -1e30)
            m_new = jnp.maximum(m, jnp.max(s, axis=-1, keepdims=True))
            alpha = jnp.exp2((m - m_new) * c)
            p = jnp.exp2((s - m_new) * c)
            l = alpha * l + jnp.sum(p, axis=-1, keepdims=True)
            v = kv_ref[pl.ds(k0, tk), j * hw + MLA_NOPE:(j + 1) * hw]
            acc = alpha * acc + jnp.dot(p.astype(BF16), v, preferred_element_type=F32)
            new.append((m_new, l, acc))
        return tuple(new)

    def body(kb, carry):
        return tile(pl.multiple_of(kb * tk, tk), carry, False)

    carry = tuple((jnp.full((tq, 1), -1e30, F32), jnp.zeros((tq, 1), F32), jnp.zeros((tq, MLA_V), F32))
                  for _ in range(hp))
    n_full = (i * tq) // tk
    carry = lax.fori_loop(0, n_full, body, carry)
    for d in range(max(1, tq // tk)):
        carry = tile(pl.multiple_of((n_full + d) * tk, tk), carry, True)
    for j in range(hp):
        m, l, acc = carry[j]
        o_ref[:, j * MLA_V:(j + 1) * MLA_V] = (acc / l).astype(BF16)


def _attention(q, kv, kr, b, s, tq=512, tk=1024, hp=2):
    n = q.shape[0]
    tq = min(tq, s)
    tk = min(tk, s)
    nq = s // tq
    kv3 = kv.reshape(b, s, kv.shape[1])
    kr3 = kr.reshape(b, s, LANE)
    return pl.pallas_call(
        functools.partial(_attn_kernel, tq=tq, tk=tk, hp=hp),
        grid=(b, MLA_HEADS // hp, nq),
        in_specs=[
            pl.BlockSpec((tq, hp * MLA_QK_PAD), lambda bi, h, i: (bi * nq + i, h)),
            pl.BlockSpec((None, s, hp * (MLA_NOPE + MLA_V)), lambda bi, h, i: (bi, 0, h)),
            pl.BlockSpec((None, s, LANE), lambda bi, h, i: (bi, 0, 0)),
        ],
        out_specs=pl.BlockSpec((tq, hp * MLA_V), lambda bi, h, i: (bi * nq + i, h)),
        out_shape=jax.ShapeDtypeStruct((n, MLA_HEADS * MLA_V), BF16),
        compiler_params=_params("parallel", "parallel", "arbitrary"),
        name="mla_attention",
    )(q, kv3, kr3)


def _ret_kernel(lg_ref, q_ref, k_ref, v_ref, g_ref, c_ref, s_ref, o_ref, state_ref, dec_ref, *, t, hp):
    first = pl.program_id(2) == 0
    c = c_ref[...]
    s = s_ref[...]
    half = RET_QK // 2
    pos = lax.broadcasted_iota(jnp.int32, (t, 1), 0).astype(F32)

    def rope(x):
        x1 = x[:, :half]
        x2 = x[:, half:]
        return jnp.concatenate([x1 * c - x2 * s, x2 * c + x1 * s], axis=1)

    for j in range(hp):
        lg = lg_ref[pl.program_id(1) * hp + j]
        cols = slice(j * RET_QK, (j + 1) * RET_QK)

        @pl.when(first)
        def _():
            state_ref[j] = jnp.zeros((RET_QK, RET_V), F32)
            ii = lax.broadcasted_iota(jnp.int32, (t, t), 0)
            jj = lax.broadcasted_iota(jnp.int32, (t, t), 1)
            dec_ref[j] = jnp.where(jj // CHUNK <= ii // CHUNK, jnp.exp(jnp.abs(ii - jj).astype(F32) * lg), 0.0)

        q = rope(q_ref[:, cols].astype(F32))
        k = rope(k_ref[:, cols].astype(F32)) * RET_QK ** -0.5
        v = v_ref[:, cols]
        q_dec = q * jnp.exp((pos + 1.0) * lg)
        k_dec = k * jnp.exp((t - 1.0 - pos) * lg)
        a = lax.dot_general(q.astype(BF16), k.astype(BF16), (((1,), (1,)), ((), ())),
                            preferred_element_type=F32) * dec_ref[j]
        state = state_ref[j]
        o = jnp.dot(a.astype(BF16), v, preferred_element_type=F32)
        o = o + jnp.dot(q_dec.astype(BF16), state.astype(BF16), preferred_element_type=F32)
        block_decay = jnp.exp(jnp.full((1, RET_V), t * 1.0, F32) * lg)
        state_ref[j] = state * block_decay + lax.dot_general(
            k_dec.astype(BF16), v, (((0,), (0,)), ((), ())), preferred_element_type=F32)

        mu = jnp.mean(o, axis=-1, keepdims=True)
        d = o - mu
        var = jnp.mean(d * d, axis=-1, keepdims=True)
        on = d * lax.rsqrt(var + RET_GN_EPS)
        g = g_ref[:, cols].astype(F32)
        o_ref[:, cols] = (g * jax.nn.sigmoid(g) * on).astype(BF16)


def _retention(zr, c_k, s_k, b, s, t=256, hp=2):
    n = zr.shape[0]
    t = min(t, s)
    nt = s // t
    ng = RET_HEADS // hp
    log_g = jnp.log1p(-(2.0 ** (-5.0 - jnp.arange(RET_HEADS, dtype=F32))))
    col = lambda part: pl.BlockSpec((t, hp * RET_QK), lambda bi, h, ti: (bi * nt + ti, part * ng + h))
    tab = pl.BlockSpec((t, LANE), lambda bi, h, ti: (bi * nt + ti, 0))
    return pl.pallas_call(
        functools.partial(_ret_kernel, t=t, hp=hp),
        grid=(b, ng, nt),
        in_specs=[pl.BlockSpec(memory_space=pltpu.SMEM), col(0), col(1), col(2), col(3), tab, tab],
        out_specs=pl.BlockSpec((t, hp * RET_V), lambda bi, h, ti: (bi * nt + ti, h)),
        out_shape=jax.ShapeDtypeStruct((n, RET_HEADS * RET_V), BF16),
        scratch_shapes=[pltpu.VMEM((hp, RET_QK, RET_V), F32), pltpu.VMEM((hp, t, t), F32)],
        compiler_params=_params("parallel", "parallel", "arbitrary"),
        name="retention",
    )(log_g, zr, zr, zr, zr, c_k, s_k)


def _gmlp_kernel(u_ref, v_ref, lng_ref, lnb_ref, ws_ref, bst_ref, o_ref, *, nblk):
    v = jax.nn.gelu(v_ref[...].astype(F32))
    mu = jnp.mean(v, axis=-1, keepdims=True)
    d = v - mu
    var = jnp.mean(d * d, axis=-1, keepdims=True)
    vn = (d * lax.rsqrt(var + EPS) * lng_ref[...] + lnb_ref[...]).astype(BF16)
    pc_i = lax.broadcasted_iota(jnp.int32, (GM_BLOCK, GM_BLOCK), 0) // CHUNK
    pc_j = lax.broadcasted_iota(jnp.int32, (GM_BLOCK, GM_BLOCK), 1) // CHUNK
    gw = GM_WIDTH // GM_GROUPS
    for g in range(GM_GROUPS):
        w = jnp.where(pc_i >= pc_j, ws_ref[g], 0.0).astype(BF16)
        bias = bst_ref[:, g:g + 1]
        for r in range(nblk):
            rows = slice(r * GM_BLOCK, (r + 1) * GM_BLOCK)
            cols = slice(g * gw, (g + 1) * gw)
            mixed = jnp.dot(w, vn[rows, cols], preferred_element_type=F32) + bias
            u = jax.nn.gelu(u_ref[rows, cols].astype(F32))
            o_ref[rows, cols] = (u * mixed).astype(BF16)


def _gmlp(zg, ln_g, ln_b, w_s, b_s_t, l, nblk=2):
    n = zg.shape[0]
    tm = nblk * GM_BLOCK
    return pl.pallas_call(
        functools.partial(_gmlp_kernel, nblk=nblk),
        grid=(n // tm,),
        in_specs=[
            pl.BlockSpec((tm, GM_WIDTH), lambda i: (i, 0)),
            pl.BlockSpec((tm, GM_WIDTH), lambda i: (i, 1)),
            pl.BlockSpec((1, GM_WIDTH), lambda i: (0, 0)),
            pl.BlockSpec((1, GM_WIDTH), lambda i: (0, 0)),
            pl.BlockSpec((None, GM_GROUPS, GM_BLOCK, GM_BLOCK), lambda i: (l, 0, 0, 0)),
            pl.BlockSpec((GM_BLOCK, GM_GROUPS), lambda i: (0, 0)),
        ],
        out_specs=pl.BlockSpec((tm, GM_WIDTH), lambda i: (i, 0)),
        out_shape=jax.ShapeDtypeStruct((n, GM_WIDTH), BF16),
        compiler_params=_params("parallel"),
        name="gmlp",
    )(zg, zg, ln_g, ln_b, w_s, b_s_t)


def _merge_kernel(h_ref, ya_ref, yb_ref, yc_ref, wg0_ref, wg1_ref, wg2_ref, bg0_ref, bg1_ref, bg2_ref,
                  wb0_ref, wb1_ref, wb2_ref, o_ref):
    h = h_ref[...]

    def branch(y_ref, wg_ref, bg_ref, wb_ref):
        gate = jax.nn.sigmoid(jnp.dot(h, wg_ref[...], preferred_element_type=F32) + bg_ref[...])
        return gate * jnp.dot(y_ref[...], wb_ref[...], preferred_element_type=F32)

    merged = branch(ya_ref, wg0_ref, bg0_ref, wb0_ref)
    merged = merged + branch(yb_ref, wg1_ref, bg1_ref, wb1_ref)
    merged = merged + branch(yc_ref, wg2_ref, bg2_ref, wb2_ref)
    o_ref[...] = merged.astype(BF16)


def _merge(h, ya, yb, yc, w_gate, b_gate, w_br, l, tm=512, tn=512):
    n, d = h.shape
    tm = min(tm, n)
    nj = d // tn
    act = pl.BlockSpec((tm, d), lambda i, j: (i, 0))
    wg = lambda br: pl.BlockSpec((None, d, tn), lambda i, j: (l, 0, br * nj + j))
    bg = lambda br: pl.BlockSpec((1, tn), lambda i, j: (0, br * nj + j))
    wb = lambda br: pl.BlockSpec((None, None, d, tn), lambda i, j: (l, br, 0, j))
    return pl.pallas_call(
        _merge_kernel,
        grid=(n // tm, nj),
        in_specs=[act, act, act, act, wg(0), wg(1), wg(2), bg(0), bg(1), bg(2), wb(0), wb(1), wb(2)],
        out_specs=pl.BlockSpec((tm, tn), lambda i, j: (i, j)),
        out_shape=jax.ShapeDtypeStruct((n, d), BF16),
        compiler_params=_params("parallel", "arbitrary"),
        name="merge",
    )(h, ya, yb, yc, w_gate, w_gate, w_gate, b_gate, b_gate, b_gate, w_br, w_br, w_br)


def _out_proj_kernel(m_ref, w_ref, g_ref, x_ref, o_ref):
    y = jnp.dot(m_ref[...], w_ref[...], preferred_element_type=F32)
    o_ref[...] = x_ref[...] + _rms(y, g_ref[...])


def _out_proj(merged, w_o, g_post, x, l, tm=512):
    n, d = x.shape
    tm = min(tm, n)
    return pl.pallas_call(
        _out_proj_kernel,
        grid=(n // tm,),
        in_specs=[
            pl.BlockSpec((tm, d), lambda i: (i, 0)),
            pl.BlockSpec((None, d, d), lambda i: (l, 0, 0)),
            pl.BlockSpec((1, d), lambda i: (0, 0)),
            pl.BlockSpec((tm, d), lambda i: (i, 0)),
        ],
        out_specs=pl.BlockSpec((tm, d), lambda i: (i, 0)),
        out_shape=jax.ShapeDtypeStruct((n, d), F32),
        compiler_params=_params("parallel"),
        name="out_proj",
    )(merged, w_o, g_post, x)


def _rot_cols(w):
    half = w.shape[-1] // 2
    return jnp.concatenate([-w[..., half:], w[..., :half]], axis=-1)


def _cast_wi_kernel(w_ref, o_ref):
    o_ref[:, :D_FF] = w_ref[...].astype(BF16)
    o_ref[:, D_FF:] = jnp.zeros((o_ref.shape[0], D_FF_PAD - D_FF), BF16)


def _cast_wi(wi, tr=256):
    nl, d, _ = wi.shape
    return pl.pallas_call(
        _cast_wi_kernel,
        grid=(nl, 2, d // tr),
        in_specs=[pl.BlockSpec((None, tr, D_FF), lambda l, h, r: (l, r, h))],
        out_specs=pl.BlockSpec((None, None, tr, D_FF_PAD), lambda l, h, r: (l, h, r, 0)),
        out_shape=jax.ShapeDtypeStruct((nl, 2, d, D_FF_PAD), BF16),
        compiler_params=_params("parallel", "parallel", "parallel"),
        name="cast_wi",
    )(wi)


def _cast_wo_kernel(w_ref, o_ref):
    rows = pl.program_id(1) * FF_TILE + lax.broadcasted_iota(jnp.int32, (FF_TILE, 1), 0)
    o_ref[...] = jnp.where(rows < D_FF, w_ref[...], 0.0).astype(BF16)


def _cast_wo(wo):
    nl, _, d = wo.shape
    spec = pl.BlockSpec((None, FF_TILE, d), lambda l, j: (l, j, 0))
    return pl.pallas_call(
        _cast_wo_kernel,
        grid=(nl, D_FF_PAD // FF_TILE),
        in_specs=[spec],
        out_specs=spec,
        out_shape=jax.ShapeDtypeStruct((nl, D_FF_PAD, d), BF16),
        compiler_params=_params("parallel", "parallel"),
        name="cast_wo",
    )(wo)


def _prep_w_in(w_in):
    o_kr = Q_LORA + KV_LORA
    o_ret = o_kr + MLA_ROPE
    o_gm = o_ret + RET_GROUP_W
    w_in = lax.optimization_barrier(w_in.astype(BF16))
    w_kr = w_in[..., o_kr:o_ret]
    w_mla = jnp.concatenate([w_in[..., :o_ret], _rot_cols(w_kr)], axis=-1)
    return w_mla, w_in[..., o_ret:o_gm], w_in[..., o_gm:]


def _prep_w_uq(w_uq):
    nl = w_uq.shape[0]
    w = w_uq.reshape(nl, Q_LORA, MLA_HEADS, MLA_NOPE + MLA_ROPE)
    w_rope = w[..., MLA_NOPE:]
    w = jnp.concatenate([w, _rot_cols(w_rope)], axis=-1)
    return w.reshape(nl, Q_LORA, MLA_HEADS * MLA_QK_PAD).astype(BF16)


def _prep_weights(p):
    w = {}
    w["ffn1"] = (_cast_wi(p["ffn1_wi"]), _cast_wo(p["ffn1_wo"]))
    w["ffn2"] = (_cast_wi(p["ffn2_wi"]), _cast_wo(p["ffn2_wo"]))
    w["w_mla"], w["w_ret"], w["w_gm"] = _prep_w_in(p["w_in"])
    w["w_uq"] = _prep_w_uq(p["w_uq"])
    for name in ("w_ukv", "w_gate", "w_br", "w_o"):
        w[name] = p[name].astype(BF16)
    w["gm_b_s_t"] = jnp.swapaxes(p["gm_b_s"], 1, 2)
    return w


def _row(v):
    return v.reshape(1, -1)


def _token_mixer(x, l, p, w, tables, b, s):
    c_r, s_r, c_k, s_k = tables
    zm, h = _norm_mm(x, _row(p["mix_pre_g"][l]), w["w_mla"], l)
    zr = _mm(h, w["w_ret"], l, tn=1024)
    zg = _mm(h, w["w_gm"], l, tn=1024)
    q, kv, kr = _mla_prep(zm, _row(p["q_norm_g"][l]), _row(p["kv_norm_g"][l]), w["w_uq"], w["w_ukv"], c_r, s_r, l)
    y_a = _attention(q, kv, kr, b, s)
    y_b = _retention(zr, c_k, s_k, b, s)
    y_c = _gmlp(zg, _row(p["gm_ln_g"][l]), _row(p["gm_ln_b"][l]), p["gm_w_s"], w["gm_b_s_t"][l], l)
    merged = _merge(h, y_a, y_b, y_c, w["w_gate"], _row(p["b_gate"][l]), w["w_br"], l)
    return _out_proj(merged, w["w_o"], _row(p["mix_post_g"][l]), x, l)


def _trunk(x, pos, p, depth):
    b, s, d = x.shape
    tables = _rope_tables(pos)
    w = _prep_weights(p)
    x = x.reshape(b * s, d)
    for l in range(depth):
        x = _ffn(x, _row(p["ffn1_pre_g"][l]), *w["ffn1"], _row(p["ffn1_post_g"][l]), l)
        x = _token_mixer(x, l, p, w, tables, b, s)
        x = _ffn(x, _row(p["ffn2_pre_g"][l]), *w["ffn2"], _row(p["ffn2_post_g"][l]), l)
    return x.reshape(b, s, d)


def kernel(x, pos, ffn1_pre_g, ffn1_wi, ffn1_wo, ffn1_post_g, mix_pre_g, w_in, q_norm_g, w_uq, kv_norm_g, w_ukv, gm_ln_g, gm_ln_b, gm_w_s, gm_b_s, w_gate, b_gate, w_br, w_o, mix_post_g, ffn2_pre_g, ffn2_wi, ffn2_wo, ffn2_post_g):
    p = dict(ffn1_pre_g=ffn1_pre_g, ffn1_wi=ffn1_wi, ffn1_wo=ffn1_wo, ffn1_post_g=ffn1_post_g, mix_pre_g=mix_pre_g,
             w_in=w_in, q_norm_g=q_norm_g, w_uq=w_uq, kv_norm_g=kv_norm_g, w_ukv=w_ukv, gm_ln_g=gm_ln_g,
             gm_ln_b=gm_ln_b, gm_w_s=gm_w_s, gm_b_s=gm_b_s, w_gate=w_gate, b_gate=b_gate, w_br=w_br, w_o=w_o,
             mix_post_g=mix_post_g, ffn2_pre_g=ffn2_pre_g, ffn2_wi=ffn2_wi, ffn2_wo=ffn2_wo, ffn2_post_g=ffn2_post_g)
    return _trunk(x, pos, p, DEPTH)
```

```python
import functools

import numpy as np
import jax
import jax.numpy as jnp
from jax import lax
from jax.experimental import pallas as pl
from jax.experimental.pallas import tpu as pltpu

F32 = jnp.float32
BF16 = jnp.bfloat16

D_MODEL = 2048
DEPTH = 4
CHUNK = 64
EPS = 1e-6
ROPE_BASE = 10000.0
MLA_HEADS = 16
MLA_NOPE = 128
MLA_ROPE = 64
MLA_V = 128
Q_LORA = 512
KV_LORA = 512
RET_HEADS = 8
RET_QK = 256
RET_V = 256
RET_GN_EPS = 1e-5
GM_GROUPS = 4
GM_WIDTH = 2048
GM_BLOCK = 128
N_BRANCH = 3
D_FF = 5504

LANE = 128
MLA_QK_PAD = 2 * LANE
MLA_GROUP_W = Q_LORA + KV_LORA + LANE
RET_GROUP_W = 4 * RET_HEADS * RET_QK
GM_GROUP_W = 2 * GM_WIDTH
FF_TILE = 512
D_FF_PAD = ((D_FF + FF_TILE - 1) // FF_TILE) * FF_TILE
VMEM_LIMIT = 56 * 2 ** 20


def _params(*sem, flags=None):
    return pltpu.CompilerParams(dimension_semantics=sem, vmem_limit_bytes=VMEM_LIMIT, flags=flags)


def _rms(x, g):
    return x * lax.rsqrt(jnp.mean(x * x, axis=-1, keepdims=True) + EPS) * g


def _rope_table_kernel(ang_r_ref, ang_k_ref, cr_ref, sr_ref, ck_ref, sk_ref):
    a = ang_r_ref[...]
    live = lax.broadcasted_iota(jnp.int32, a.shape, 1) < MLA_ROPE
    cr_ref[...] = jnp.where(live, jnp.cos(a), 0.0)
    sr_ref[...] = jnp.where(live, jnp.sin(a), 0.0)
    k = ang_k_ref[...]
    ck_ref[...] = jnp.cos(k)
    sk_ref[...] = jnp.sin(k)


def _rope_tables(pos):
    n = pos.size
    p = pos.astype(F32).reshape(n, 1)
    inv_r = ROPE_BASE ** (-jnp.arange(0, MLA_ROPE, 2, dtype=F32) / MLA_ROPE)
    inv_k = ROPE_BASE ** (-jnp.arange(0, RET_QK, 2, dtype=F32) / RET_QK)
    ang_r = p * inv_r
    ang_r = jnp.concatenate([ang_r, ang_r, jnp.zeros((n, LANE - MLA_ROPE), F32)], axis=1)
    ang_k = p * inv_k
    tm = min(n, 1024)
    spec = pl.BlockSpec((tm, LANE), lambda i: (i, 0))
    out = jax.ShapeDtypeStruct((n, LANE), F32)
    return pl.pallas_call(
        _rope_table_kernel,
        grid=(n // tm,),
        in_specs=[spec, spec],
        out_specs=[spec] * 4,
        out_shape=[out] * 4,
        compiler_params=_params("parallel"),
        name="rope_tables",
    )(ang_r, ang_k)


def _ffn_kernel(x_ref, gpre_ref, wa_ref, wb_ref, wo_ref, gpost_ref, o_ref, h_ref):
    f = pl.program_id(1)

    def hidden_tile(h):
        a = jnp.dot(h, wa_ref[...], preferred_element_type=F32)
        b = jnp.dot(h, wb_ref[...], preferred_element_type=F32)
        act = (a * jax.nn.sigmoid(a) * b).astype(BF16)
        return jnp.dot(act, wo_ref[...], preferred_element_type=F32)

    @pl.when(f == 0)
    def _():
        h = _rms(x_ref[...], gpre_ref[...]).astype(BF16)
        h_ref[...] = h
        o_ref[...] = hidden_tile(h)

    @pl.when(f > 0)
    def _():
        o_ref[...] += hidden_tile(h_ref[...])

    @pl.when(f == pl.num_programs(1) - 1)
    def _():
        o_ref[...] = x_ref[...] + _rms(o_ref[...], 0.5 * gpost_ref[...])


def _ffn(x, g_pre, wi, wo, g_post, l, tm=1024):
    n, d = x.shape
    nf = D_FF_PAD // FF_TILE
    tm = min(tm, n)
    return pl.pallas_call(
        _ffn_kernel,
        grid=(n // tm, nf),
        in_specs=[
            pl.BlockSpec((tm, d), lambda i, f: (i, 0)),
            pl.BlockSpec((1, d), lambda i, f: (0, 0)),
            pl.BlockSpec((None, None, d, FF_TILE), lambda i, f: (l, 0, 0, f)),
            pl.BlockSpec((None, None, d, FF_TILE), lambda i, f: (l, 1, 0, f)),
            pl.BlockSpec((None, FF_TILE, d), lambda i, f: (l, f, 0)),
            pl.BlockSpec((1, d), lambda i, f: (0, 0)),
        ],
        out_specs=pl.BlockSpec((tm, d), lambda i, f: (i, 0)),
        out_shape=jax.ShapeDtypeStruct((n, d), F32),
        scratch_shapes=[pltpu.VMEM((tm, d), BF16)],
        compiler_params=_params("parallel", "arbitrary"),
        name="ffn",
    )(x, g_pre, wi, wi, wo, g_post)


def _norm_mm_kernel(x_ref, g_ref, w_ref, o_ref, h_ref):
    h = _rms(x_ref[...], g_ref[...]).astype(BF16)
    h_ref[...] = h
    o_ref[...] = jnp.dot(h, w_ref[...], preferred_element_type=F32).astype(o_ref.dtype)


def _norm_mm(x, g, w, l, tm=1024):
    n, k = x.shape
    nc = w.shape[2]
    tm = min(tm, n)
    return pl.pallas_call(
        _norm_mm_kernel,
        grid=(n // tm,),
        in_specs=[pl.BlockSpec((tm, k), lambda i: (i, 0)), pl.BlockSpec((1, k), lambda i: (0, 0)),
                  pl.BlockSpec((None, k, nc), lambda i: (l, 0, 0))],
        out_specs=[pl.BlockSpec((tm, nc), lambda i: (i, 0)), pl.BlockSpec((tm, k), lambda i: (i, 0))],
        out_shape=[jax.ShapeDtypeStruct((n, nc), BF16), jax.ShapeDtypeStruct((n, k), BF16)],
        compiler_params=_params("parallel"),
        name="norm_in_proj",
    )(x, g, w)


def _mm_kernel(a_ref, w_ref, o_ref):
    o_ref[...] = jnp.dot(a_ref[...], w_ref[...], preferred_element_type=F32).astype(o_ref.dtype)


def _mm(a, w, l, col0, nc, tn=1024, tm=2048):
    n, k = a.shape
    j0 = col0 // tn
    tm = min(tm, n)
    return pl.pallas_call(
        _mm_kernel,
        grid=(n // tm, nc // tn),
        in_specs=[pl.BlockSpec((tm, k), lambda i, j: (i, 0)), pl.BlockSpec((None, k, tn), lambda i, j: (l, 0, j0 + j))],
        out_specs=pl.BlockSpec((tm, tn), lambda i, j: (i, j)),
        out_shape=jax.ShapeDtypeStruct((n, nc), BF16),
        compiler_params=_params("parallel", "arbitrary"),
        name="in_proj",
    )(a, w)


def _rope_half_block(blk, c, s):
    return blk * c + pltpu.roll(blk, MLA_ROPE, axis=1) * s


def _mla_prep_kernel(z_ref, gq_ref, gkv_ref, wq_ref, wkv_ref, c_ref, s_ref, q_ref, kv_ref, kr_ref):
    c = c_ref[...]
    s = s_ref[...]
    hq = _rms(z_ref[:, :Q_LORA].astype(F32), gq_ref[...]).astype(BF16)
    for h in range(MLA_HEADS):
        lo = h * MLA_QK_PAD
        qh = jnp.dot(hq, wq_ref[:, lo:lo + MLA_QK_PAD], preferred_element_type=F32)
        q_ref[:, lo:lo + LANE] = qh[:, :LANE].astype(BF16)
        q_ref[:, lo + LANE:lo + MLA_QK_PAD] = _rope_half_block(qh[:, LANE:], c, s).astype(BF16)
    hkv = _rms(z_ref[:, Q_LORA:Q_LORA + KV_LORA].astype(F32), gkv_ref[...]).astype(BF16)
    step = 4 * (MLA_NOPE + MLA_V)
    for lo in range(0, MLA_HEADS * (MLA_NOPE + MLA_V), step):
        kv_ref[:, lo:lo + step] = jnp.dot(hkv, wkv_ref[:, lo:lo + step], preferred_element_type=F32).astype(BF16)
    kr_ref[...] = _rope_half_block(z_ref[:, Q_LORA + KV_LORA:].astype(F32), c, s).astype(BF16)


def _mla_prep(zm, gq, gkv, wq, wkv, c_r, s_r, l, tm=256):
    n = zm.shape[0]
    tm = min(tm, n)
    wq_w = MLA_HEADS * MLA_QK_PAD
    wkv_w = MLA_HEADS * (MLA_NOPE + MLA_V)
    row = lambda w: pl.BlockSpec((tm, w), lambda i: (i, 0))
    full = lambda r, w: pl.BlockSpec((r, w), lambda i: (0, 0))
    layer = lambda r, w: pl.BlockSpec((None, r, w), lambda i: (l, 0, 0))
    return pl.pallas_call(
        _mla_prep_kernel,
        grid=(n // tm,),
        in_specs=[row(MLA_GROUP_W), full(1, Q_LORA), full(1, KV_LORA), layer(Q_LORA, wq_w), layer(KV_LORA, wkv_w),
                  row(LANE), row(LANE)],
        out_specs=[row(wq_w), row(wkv_w), row(LANE)],
        out_shape=[jax.ShapeDtypeStruct((n, wq_w), BF16), jax.ShapeDtypeStruct((n, wkv_w), BF16),
                   jax.ShapeDtypeStruct((n, LANE), BF16)],
        compiler_params=_params("parallel"),
        name="mla_prep",
    )(zm, gq, gkv, wq, wkv, c_r, s_r)


def _attn_kernel(q_ref, kv_ref, kr_ref, o_ref, *, tq, tk, hp):
    i = pl.program_id(2)
    c = (MLA_NOPE + MLA_ROPE) ** -0.5 * np.log2(np.e)
    hw = MLA_NOPE + MLA_V

    def tile(k0, carry, masked):
        kr = kr_ref[pl.ds(k0, tk), :]
        if masked:
            qc = (i * tq + lax.broadcasted_iota(jnp.int32, (tq, tk), 0)) // CHUNK
            kc = (k0 + lax.broadcasted_iota(jnp.int32, (tq, tk), 1)) // CHUNK
            mask = kc <= qc
        new = []
        for j in range(hp):
            m, l, acc = carry[j]
            q = q_ref[:, j * MLA_QK_PAD:(j + 1) * MLA_QK_PAD]
            k = jnp.concatenate([kv_ref[pl.ds(k0, tk), j * hw:j * hw + MLA_NOPE], kr], axis=1)
            s = lax.dot_general(q, k, (((1,), (1,)), ((), ())), preferred_element_type=F32)
            if masked:
                s = jnp.where(mask, s, -1e30)
            m_new = jnp.maximum(m, jnp.max(s, axis=-1, keepdims=True))
            alpha = jnp.exp2((m - m_new) * c)
            p = jnp.exp2((s - m_new) * c)
            l = alpha * l + jnp.sum(p, axis=-1, keepdims=True)
            v = kv_ref[pl.ds(k0, tk), j * hw + MLA_NOPE:(j + 1) * hw]
            acc = alpha * acc + jnp.dot(p.astype(BF16), v, preferred_element_type=F32)
            new.append((m_new, l, acc))
        return tuple(new)

    def body(kb, carry):
        return tile(pl.multiple_of(kb * tk, tk), carry, False)

    carry = tuple((jnp.full((tq, 1), -1e30, F32), jnp.zeros((tq, 1), F32), jnp.zeros((tq, MLA_V), F32))
                  for _ in range(hp))
    n_full = (i * tq) // tk
    carry = lax.fori_loop(0, n_full, body, carry)
    for d in range(max(1, tq // tk)):
        carry = tile(pl.multiple_of((n_full + d) * tk, tk), carry, True)
    for j in range(hp):
        m, l, acc = carry[j]
        o_ref[:, j * MLA_V:(j + 1) * MLA_V] = (acc / l).astype(BF16)


def _attention(q, kv, kr, b, s, tq=512, tk=1024, hp=2):
    n = q.shape[0]
    tq = min(tq, s)
    tk = min(tk, s)
    nq = s // tq
    kv3 = kv.reshape(b, s, kv.shape[1])
    kr3 = kr.reshape(b, s, LANE)
    return pl.pallas_call(
        functools.partial(_attn_kernel, tq=tq, tk=tk, hp=hp),
        grid=(b, MLA_HEADS // hp, nq),
        in_specs=[
            pl.BlockSpec((tq, hp * MLA_QK_PAD), lambda bi, h, i: (bi * nq + i, h)),
            pl.BlockSpec((None, s, hp * (MLA_NOPE + MLA_V)), lambda bi, h, i: (bi, 0, h)),
            pl.BlockSpec((None, s, LANE), lambda bi, h, i: (bi, 0, 0)),
        ],
        out_specs=pl.BlockSpec((tq, hp * MLA_V), lambda bi, h, i: (bi * nq + i, h)),
        out_shape=jax.ShapeDtypeStruct((n, MLA_HEADS * MLA_V), BF16),
        compiler_params=_params("parallel", "parallel", "arbitrary"),
        name="mla_attention",
    )(q, kv3, kr3)


def _ret_kernel(lg_ref, q_ref, k_ref, v_ref, g_ref, c_ref, s_ref, o_ref, state_ref, dec_ref, *, t, hp):
    first = pl.program_id(2) == 0
    c = c_ref[...]
    s = s_ref[...]
    half = RET_QK // 2
    pos = lax.broadcasted_iota(jnp.int32, (t, 1), 0).astype(F32)

    def rope(x):
        x1 = x[:, :half]
        x2 = x[:, half:]
        return jnp.concatenate([x1 * c - x2 * s, x2 * c + x1 * s], axis=1)

    for j in range(hp):
        lg = lg_ref[pl.program_id(1) * hp + j]
        cols = slice(j * RET_QK, (j + 1) * RET_QK)

        @pl.when(first)
        def _():
            state_ref[j] = jnp.zeros((RET_QK, RET_V), F32)
            ii = lax.broadcasted_iota(jnp.int32, (t, t), 0)
            jj = lax.broadcasted_iota(jnp.int32, (t, t), 1)
            dec_ref[j] = jnp.where(jj // CHUNK <= ii // CHUNK, jnp.exp(jnp.abs(ii - jj).astype(F32) * lg), 0.0)

        q = rope(q_ref[:, cols].astype(F32))
        k = rope(k_ref[:, cols].astype(F32)) * RET_QK ** -0.5
        v = v_ref[:, cols]
        q_dec = q * jnp.exp((pos + 1.0) * lg)
        k_dec = k * jnp.exp((t - 1.0 - pos) * lg)
        a = lax.dot_general(q.astype(BF16), k.astype(BF16), (((1,), (1,)), ((), ())),
                            preferred_element_type=F32) * dec_ref[j]
        state = state_ref[j]
        o = jnp.dot(a.astype(BF16), v, preferred_element_type=F32)
        o = o + jnp.dot(q_dec.astype(BF16), state.astype(BF16), preferred_element_type=F32)
        block_decay = jnp.exp(jnp.full((1, RET_V), t * 1.0, F32) * lg)
        state_ref[j] = state * block_decay + lax.dot_general(
            k_dec.astype(BF16), v, (((0,), (0,)), ((), ())), preferred_element_type=F32)

        mu = jnp.mean(o, axis=-1, keepdims=True)
        d = o - mu
        var = jnp.mean(d * d, axis=-1, keepdims=True)
        on = d * lax.rsqrt(var + RET_GN_EPS)
        g = g_ref[:, cols].astype(F32)
        o_ref[:, cols] = (g * jax.nn.sigmoid(g) * on).astype(BF16)


def _retention(zr, c_k, s_k, b, s, t=512, hp=2):
    n = zr.shape[0]
    t = min(t, s)
    nt = s // t
    ng = RET_HEADS // hp
    log_g = jnp.log1p(-(2.0 ** (-5.0 - jnp.arange(RET_HEADS, dtype=F32))))
    col = lambda part: pl.BlockSpec((t, hp * RET_QK), lambda bi, h, ti: (bi * nt + ti, part * ng + h))
    tab = pl.BlockSpec((t, LANE), lambda bi, h, ti: (bi * nt + ti, 0))
    return pl.pallas_call(
        functools.partial(_ret_kernel, t=t, hp=hp),
        grid=(b, ng, nt),
        in_specs=[pl.BlockSpec(memory_space=pltpu.SMEM), col(0), col(1), col(2), col(3), tab, tab],
        out_specs=pl.BlockSpec((t, hp * RET_V), lambda bi, h, ti: (bi * nt + ti, h)),
        out_shape=jax.ShapeDtypeStruct((n, RET_HEADS * RET_V), BF16),
        scratch_shapes=[pltpu.VMEM((hp, RET_QK, RET_V), F32), pltpu.VMEM((hp, t, t), F32)],
        compiler_params=_params("parallel", "parallel", "arbitrary"),
        name="retention",
    )(log_g, zr, zr, zr, zr, c_k, s_k)


def _gmlp_kernel(u_ref, v_ref, lng_ref, lnb_ref, ws_ref, bst_ref, o_ref, *, nblk):
    v = jax.nn.gelu(v_ref[...].astype(F32))
    mu = jnp.mean(v, axis=-1, keepdims=True)
    d = v - mu
    var = jnp.mean(d * d, axis=-1, keepdims=True)
    vn = (d * lax.rsqrt(var + EPS) * lng_ref[...] + lnb_ref[...]).astype(BF16)
    pc_i = lax.broadcasted_iota(jnp.int32, (GM_BLOCK, GM_BLOCK), 0) // CHUNK
    pc_j = lax.broadcasted_iota(jnp.int32, (GM_BLOCK, GM_BLOCK), 1) // CHUNK
    gw = GM_WIDTH // GM_GROUPS
    for g in range(GM_GROUPS):
        w = jnp.where(pc_i >= pc_j, ws_ref[g], 0.0).astype(BF16)
        bias = bst_ref[:, g:g + 1]
        for r in range(nblk):
            rows = slice(r * GM_BLOCK, (r + 1) * GM_BLOCK)
            cols = slice(g * gw, (g + 1) * gw)
            mixed = jnp.dot(w, vn[rows, cols], preferred_element_type=F32) + bias
            u = jax.nn.gelu(u_ref[rows, cols].astype(F32))
            o_ref[rows, cols] = (u * mixed).astype(BF16)


def _gmlp(zg, ln_g, ln_b, w_s, b_s_t, l, nblk=2):
    n = zg.shape[0]
    tm = nblk * GM_BLOCK
    return pl.pallas_call(
        functools.partial(_gmlp_kernel, nblk=nblk),
        grid=(n // tm,),
        in_specs=[
            pl.BlockSpec((tm, GM_WIDTH), lambda i: (i, 0)),
            pl.BlockSpec((tm, GM_WIDTH), lambda i: (i, 1)),
            pl.BlockSpec((1, GM_WIDTH), lambda i: (0, 0)),
            pl.BlockSpec((1, GM_WIDTH), lambda i: (0, 0)),
            pl.BlockSpec((None, GM_GROUPS, GM_BLOCK, GM_BLOCK), lambda i: (l, 0, 0, 0)),
            pl.BlockSpec((GM_BLOCK, GM_GROUPS), lambda i: (0, 0)),
        ],
        out_specs=pl.BlockSpec((tm, GM_WIDTH), lambda i: (i, 0)),
        out_shape=jax.ShapeDtypeStruct((n, GM_WIDTH), BF16),
        compiler_params=_params("parallel"),
        name="gmlp",
    )(zg, zg, ln_g, ln_b, w_s, b_s_t)


def _merge_kernel(h_ref, ya_ref, yb_ref, yc_ref, wg0_ref, wg1_ref, wg2_ref, bg0_ref, bg1_ref, bg2_ref,
                  wb0_ref, wb1_ref, wb2_ref, o_ref):
    h = h_ref[...]

    def branch(y_ref, wg_ref, bg_ref, wb_ref):
        gate = jax.nn.sigmoid(jnp.dot(h, wg_ref[...], preferred_element_type=F32) + bg_ref[...])
        return gate * jnp.dot(y_ref[...], wb_ref[...], preferred_element_type=F32)

    merged = branch(ya_ref, wg0_ref, bg0_ref, wb0_ref)
    merged = merged + branch(yb_ref, wg1_ref, bg1_ref, wb1_ref)
    merged = merged + branch(yc_ref, wg2_ref, bg2_ref, wb2_ref)
    o_ref[...] = merged.astype(BF16)


def _merge(h, ya, yb, yc, w_gate, b_gate, w_br, l, tm=512, tn=512):
    n, d = h.shape
    tm = min(tm, n)
    nj = d // tn
    act = pl.BlockSpec((tm, d), lambda i, j: (i, 0))
    wg = lambda br: pl.BlockSpec((None, d, tn), lambda i, j: (l, 0, br * nj + j))
    bg = lambda br: pl.BlockSpec((1, tn), lambda i, j: (0, br * nj + j))
    wb = lambda br: pl.BlockSpec((None, None, d, tn), lambda i, j: (l, br, 0, j))
    return pl.pallas_call(
        _merge_kernel,
        grid=(n // tm, nj),
        in_specs=[act, act, act, act, wg(0), wg(1), wg(2), bg(0), bg(1), bg(2), wb(0), wb(1), wb(2)],
        out_specs=pl.BlockSpec((tm, tn), lambda i, j: (i, j)),
        out_shape=jax.ShapeDtypeStruct((n, d), BF16),
        compiler_params=_params("parallel", "arbitrary"),
        name="merge",
    )(h, ya, yb, yc, w_gate, w_gate, w_gate, b_gate, b_gate, b_gate, w_br, w_br, w_br)


def _out_proj_kernel(m_ref, w_ref, g_ref, x_ref, o_ref):
    y = jnp.dot(m_ref[...], w_ref[...], preferred_element_type=F32)
    o_ref[...] = x_ref[...] + _rms(y, g_ref[...])


def _out_proj(merged, w_o, g_post, x, l, tm=512):
    n, d = x.shape
    tm = min(tm, n)
    return pl.pallas_call(
        _out_proj_kernel,
        grid=(n // tm,),
        in_specs=[
            pl.BlockSpec((tm, d), lambda i: (i, 0)),
            pl.BlockSpec((None, d, d), lambda i: (l, 0, 0)),
            pl.BlockSpec((1, d), lambda i: (0, 0)),
            pl.BlockSpec((tm, d), lambda i: (i, 0)),
        ],
        out_specs=pl.BlockSpec((tm, d), lambda i: (i, 0)),
        out_shape=jax.ShapeDtypeStruct((n, d), F32),
        compiler_params=_params("parallel"),
        name="out_proj",
    )(merged, w_o, g_post, x)


def _rot_cols(w):
    half = w.shape[-1] // 2
    return jnp.concatenate([-w[..., half:], w[..., :half]], axis=-1)


def _cast_wi_kernel(w_ref, o_ref):
    o_ref[:, :D_FF] = w_ref[...].astype(BF16)
    o_ref[:, D_FF:] = jnp.zeros((o_ref.shape[0], D_FF_PAD - D_FF), BF16)


def _cast_wi(wi, tr=256):
    nl, d, _ = wi.shape
    return pl.pallas_call(
        _cast_wi_kernel,
        grid=(nl, 2, d // tr),
        in_specs=[pl.BlockSpec((None, tr, D_FF), lambda l, h, r: (l, r, h))],
        out_specs=pl.BlockSpec((None, None, tr, D_FF_PAD), lambda l, h, r: (l, h, r, 0)),
        out_shape=jax.ShapeDtypeStruct((nl, 2, d, D_FF_PAD), BF16),
        compiler_params=_params("parallel", "parallel", "parallel"),
        name="cast_wi",
    )(wi)


def _cast_wo_kernel(w_ref, o_ref):
    rows = pl.program_id(1) * FF_TILE + lax.broadcasted_iota(jnp.int32, (FF_TILE, 1), 0)
    o_ref[...] = jnp.where(rows < D_FF, w_ref[...], 0.0).astype(BF16)


def _cast_wo(wo):
    nl, _, d = wo.shape
    spec = pl.BlockSpec((None, FF_TILE, d), lambda l, j: (l, j, 0))
    return pl.pallas_call(
        _cast_wo_kernel,
        grid=(nl, D_FF_PAD // FF_TILE),
        in_specs=[spec],
        out_specs=spec,
        out_shape=jax.ShapeDtypeStruct((nl, D_FF_PAD, d), BF16),
        compiler_params=_params("parallel", "parallel"),
        name="cast_wo",
    )(wo)


W_IN_SHIFT = MLA_ROPE
W_IN_TILE = 1024


def _cast_w_in_kernel(a_ref, b_ref, o_ref):
    o_ref[...] = jnp.concatenate([a_ref[:, W_IN_SHIFT:], b_ref[:, :W_IN_SHIFT]], axis=1).astype(BF16)


def _cast_w_in(w_in, tr=512):
    nl, d, _ = w_in.shape
    first = (Q_LORA + KV_LORA) // W_IN_TILE
    per_tile = W_IN_TILE // LANE
    return pl.pallas_call(
        _cast_w_in_kernel,
        grid=(nl, d // tr, (RET_GROUP_W + GM_GROUP_W) // W_IN_TILE),
        in_specs=[pl.BlockSpec((None, tr, W_IN_TILE), lambda l, r, j: (l, r, first + j)),
                  pl.BlockSpec((None, tr, LANE), lambda l, r, j: (l, r, (first + j + 1) * per_tile))],
        out_specs=pl.BlockSpec((None, tr, W_IN_TILE), lambda l, r, j: (l, r, j)),
        out_shape=jax.ShapeDtypeStruct((nl, d, RET_GROUP_W + GM_GROUP_W), BF16),
        compiler_params=_params("parallel", "parallel", "parallel"),
        name="cast_w_in",
    )(w_in, w_in)


def _prep_w_mla(w_in):
    o_kr = Q_LORA + KV_LORA
    o_ret = o_kr + MLA_ROPE
    return jnp.concatenate([w_in[..., :o_ret], _rot_cols(w_in[..., o_kr:o_ret])], axis=-1).astype(BF16)


def _prep_w_uq(w_uq):
    nl = w_uq.shape[0]
    w = w_uq.reshape(nl, Q_LORA, MLA_HEADS, MLA_NOPE + MLA_ROPE)
    w_rope = w[..., MLA_NOPE:]
    w = jnp.concatenate([w, _rot_cols(w_rope)], axis=-1)
    return w.reshape(nl, Q_LORA, MLA_HEADS * MLA_QK_PAD).astype(BF16)


def _prep_weights(p):
    w = {}
    w["ffn1"] = (_cast_wi(p["ffn1_wi"]), _cast_wo(p["ffn1_wo"]))
    w["ffn2"] = (_cast_wi(p["ffn2_wi"]), _cast_wo(p["ffn2_wo"]))
    w["w_mla"] = _prep_w_mla(p["w_in"])
    w["w_rg"] = _cast_w_in(p["w_in"])
    w["w_uq"] = _prep_w_uq(p["w_uq"])
    for name in ("w_ukv", "w_gate", "w_br", "w_o"):
        w[name] = p[name].astype(BF16)
    w["gm_b_s_t"] = jnp.swapaxes(p["gm_b_s"], 1, 2)
    return w


def _row(v):
    return v.reshape(1, -1)


def _token_mixer(x, l, p, w, tables, b, s):
    c_r, s_r, c_k, s_k = tables
    zm, h = _norm_mm(x, _row(p["mix_pre_g"][l]), w["w_mla"], l)
    zr = _mm(h, w["w_rg"], l, 0, RET_GROUP_W)
    zg = _mm(h, w["w_rg"], l, RET_GROUP_W, GM_GROUP_W)
    q, kv, kr = _mla_prep(zm, _row(p["q_norm_g"][l]), _row(p["kv_norm_g"][l]), w["w_uq"], w["w_ukv"], c_r, s_r, l)
    y_a = _attention(q, kv, kr, b, s)
    y_b = _retention(zr, c_k, s_k, b, s)
    y_c = _gmlp(zg, _row(p["gm_ln_g"][l]), _row(p["gm_ln_b"][l]), p["gm_w_s"], w["gm_b_s_t"][l], l)
    merged = _merge(h, y_a, y_b, y_c, w["w_gate"], _row(p["b_gate"][l]), w["w_br"], l)
    return _out_proj(merged, w["w_o"], _row(p["mix_post_g"][l]), x, l)


def _trunk(x, pos, p, depth):
    b, s, d = x.shape
    tables = _rope_tables(pos)
    w = _prep_weights(p)
    x = x.reshape(b * s, d)
    for l in range(depth):
        x = _ffn(x, _row(p["ffn1_pre_g"][l]), *w["ffn1"], _row(p["ffn1_post_g"][l]), l)
        x = _token_mixer(x, l, p, w, tables, b, s)
        x = _ffn(x, _row(p["ffn2_pre_g"][l]), *w["ffn2"], _row(p["ffn2_post_g"][l]), l)
    return x.reshape(b, s, d)


def kernel(x, pos, ffn1_pre_g, ffn1_wi, ffn1_wo, ffn1_post_g, mix_pre_g, w_in, q_norm_g, w_uq, kv_norm_g, w_ukv, gm_ln_g, gm_ln_b, gm_w_s, gm_b_s, w_gate, b_gate, w_br, w_o, mix_post_g, ffn2_pre_g, ffn2_wi, ffn2_wo, ffn2_post_g):
    p = dict(ffn1_pre_g=ffn1_pre_g, ffn1_wi=ffn1_wi, ffn1_wo=ffn1_wo, ffn1_post_g=ffn1_post_g, mix_pre_g=mix_pre_g,
             w_in=w_in, q_norm_g=q_norm_g, w_uq=w_uq, kv_norm_g=kv_norm_g, w_ukv=w_ukv, gm_ln_g=gm_ln_g,
             gm_ln_b=gm_ln_b, gm_w_s=gm_w_s, gm_b_s=gm_b_s, w_gate=w_gate, b_gate=b_gate, w_br=w_br, w_o=w_o,
             mix_post_g=mix_post_g, ffn2_pre_g=ffn2_pre_g, ffn2_wi=ffn2_wi, ffn2_wo=ffn2_wo, ffn2_post_g=ffn2_post_g)
    return _trunk(x, pos, p, DEPTH)
```

```python
import functools

import numpy as np
import jax
import jax.numpy as jnp
from jax import lax
from jax.experimental import pallas as pl
from jax.experimental.pallas import tpu as pltpu

F32 = jnp.float32
BF16 = jnp.bfloat16

D_MODEL = 2048
DEPTH = 4
CHUNK = 64
EPS = 1e-6
ROPE_BASE = 10000.0
MLA_HEADS = 16
MLA_NOPE = 128
MLA_ROPE = 64
MLA_V = 128
Q_LORA = 512
KV_LORA = 512
RET_HEADS = 8
RET_QK = 256
RET_V = 256
RET_GN_EPS = 1e-5
GM_GROUPS = 4
GM_WIDTH = 2048
GM_BLOCK = 128
N_BRANCH = 3
D_FF = 5504

LANE = 128
MLA_QK_PAD = 2 * LANE
MLA_GROUP_W = Q_LORA + KV_LORA + LANE
RET_GROUP_W = 4 * RET_HEADS * RET_QK
GM_GROUP_W = 2 * GM_WIDTH
FF_TILE = 512
D_FF_PAD = ((D_FF + FF_TILE - 1) // FF_TILE) * FF_TILE
VMEM_LIMIT = 56 * 2 ** 20


def _params(*sem, flags=None):
    return pltpu.CompilerParams(dimension_semantics=sem, vmem_limit_bytes=VMEM_LIMIT, flags=flags)


def _rms(x, g):
    return x * lax.rsqrt(jnp.mean(x * x, axis=-1, keepdims=True) + EPS) * g


def _rope_table_kernel(ang_r_ref, ang_k_ref, cr_ref, sr_ref, ck_ref, sk_ref):
    a = ang_r_ref[...]
    live = lax.broadcasted_iota(jnp.int32, a.shape, 1) < MLA_ROPE
    cr_ref[...] = jnp.where(live, jnp.cos(a), 0.0)
    sr_ref[...] = jnp.where(live, jnp.sin(a), 0.0)
    k = ang_k_ref[...]
    ck_ref[...] = jnp.cos(k)
    sk_ref[...] = jnp.sin(k)


def _rope_tables(pos):
    n = pos.size
    p = pos.astype(F32).reshape(n, 1)
    inv_r = ROPE_BASE ** (-jnp.arange(0, MLA_ROPE, 2, dtype=F32) / MLA_ROPE)
    inv_k = ROPE_BASE ** (-jnp.arange(0, RET_QK, 2, dtype=F32) / RET_QK)
    ang_r = p * inv_r
    ang_r = jnp.concatenate([ang_r, ang_r, jnp.zeros((n, LANE - MLA_ROPE), F32)], axis=1)
    ang_k = p * inv_k
    tm = min(n, 1024)
    spec = pl.BlockSpec((tm, LANE), lambda i: (i, 0))
    out = jax.ShapeDtypeStruct((n, LANE), F32)
    return pl.pallas_call(
        _rope_table_kernel,
        grid=(n // tm,),
        in_specs=[spec, spec],
        out_specs=[spec] * 4,
        out_shape=[out] * 4,
        compiler_params=_params("parallel"),
        name="rope_tables",
    )(ang_r, ang_k)


def _ffn_kernel(x_ref, gpre_ref, wa_ref, wb_ref, wo_ref, gpost_ref, o_ref, h_ref):
    f = pl.program_id(1)

    def hidden_tile(h):
        a = jnp.dot(h, wa_ref[...], preferred_element_type=F32)
        b = jnp.dot(h, wb_ref[...], preferred_element_type=F32)
        act = (a * jax.nn.sigmoid(a) * b).astype(BF16)
        return jnp.dot(act, wo_ref[...], preferred_element_type=F32)

    @pl.when(f == 0)
    def _():
        h = _rms(x_ref[...], gpre_ref[...]).astype(BF16)
        h_ref[...] = h
        o_ref[...] = hidden_tile(h)

    @pl.when(f > 0)
    def _():
        o_ref[...] += hidden_tile(h_ref[...])

    @pl.when(f == pl.num_programs(1) - 1)
    def _():
        o_ref[...] = x_ref[...] + _rms(o_ref[...], 0.5 * gpost_ref[...])


def _ffn(x, g_pre, wi, wo, g_post, l, tm=1024):
    n, d = x.shape
    nf = D_FF_PAD // FF_TILE
    tm = min(tm, n)
    return pl.pallas_call(
        _ffn_kernel,
        grid=(n // tm, nf),
        in_specs=[
            pl.BlockSpec((tm, d), lambda i, f: (i, 0)),
            pl.BlockSpec((1, d), lambda i, f: (0, 0)),
            pl.BlockSpec((None, None, d, FF_TILE), lambda i, f: (l, 0, 0, f)),
            pl.BlockSpec((None, None, d, FF_TILE), lambda i, f: (l, 1, 0, f)),
            pl.BlockSpec((None, FF_TILE, d), lambda i, f: (l, f, 0)),
            pl.BlockSpec((1, d), lambda i, f: (0, 0)),
        ],
        out_specs=pl.BlockSpec((tm, d), lambda i, f: (i, 0)),
        out_shape=jax.ShapeDtypeStruct((n, d), F32),
        scratch_shapes=[pltpu.VMEM((tm, d), BF16)],
        compiler_params=_params("parallel", "arbitrary"),
        name="ffn",
    )(x, g_pre, wi, wi, wo, g_post)


def _norm_mm_kernel(x_ref, g_ref, w_ref, o_ref, h_ref):
    h = _rms(x_ref[...], g_ref[...]).astype(BF16)
    h_ref[...] = h
    o_ref[...] = jnp.dot(h, w_ref[...], preferred_element_type=F32).astype(o_ref.dtype)


def _norm_mm(x, g, w, l, tm=1024):
    n, k = x.shape
    nc = w.shape[2]
    tm = min(tm, n)
    return pl.pallas_call(
        _norm_mm_kernel,
        grid=(n // tm,),
        in_specs=[pl.BlockSpec((tm, k), lambda i: (i, 0)), pl.BlockSpec((1, k), lambda i: (0, 0)),
                  pl.BlockSpec((None, k, nc), lambda i: (l, 0, 0))],
        out_specs=[pl.BlockSpec((tm, nc), lambda i: (i, 0)), pl.BlockSpec((tm, k), lambda i: (i, 0))],
        out_shape=[jax.ShapeDtypeStruct((n, nc), BF16), jax.ShapeDtypeStruct((n, k), BF16)],
        compiler_params=_params("parallel"),
        name="norm_in_proj",
    )(x, g, w)


def _mm_kernel(a_ref, w_ref, o_ref):
    o_ref[...] = jnp.dot(a_ref[...], w_ref[...], preferred_element_type=F32).astype(o_ref.dtype)


def _mm(a, w, l, col0, nc, tn=1024, tm=2048):
    n, k = a.shape
    j0 = col0 // tn
    tm = min(tm, n)
    return pl.pallas_call(
        _mm_kernel,
        grid=(n // tm, nc // tn),
        in_specs=[pl.BlockSpec((tm, k), lambda i, j: (i, 0)), pl.BlockSpec((None, k, tn), lambda i, j: (l, 0, j0 + j))],
        out_specs=pl.BlockSpec((tm, tn), lambda i, j: (i, j)),
        out_shape=jax.ShapeDtypeStruct((n, nc), BF16),
        compiler_params=_params("parallel", "arbitrary"),
        name="in_proj",
    )(a, w)


def _rope_half_block(blk, c, s):
    return blk * c + pltpu.roll(blk, MLA_ROPE, axis=1) * s


def _mla_prep_kernel(z_ref, gq_ref, gkv_ref, wq_ref, wkv_ref, c_ref, s_ref, q_ref, kv_ref, kr_ref):
    c = c_ref[...]
    s = s_ref[...]
    hq = _rms(z_ref[:, :Q_LORA].astype(F32), gq_ref[...]).astype(BF16)
    for h in range(MLA_HEADS):
        lo = h * MLA_QK_PAD
        qh = jnp.dot(hq, wq_ref[:, lo:lo + MLA_QK_PAD], preferred_element_type=F32)
        q_ref[:, lo:lo + LANE] = qh[:, :LANE].astype(BF16)
        q_ref[:, lo + LANE:lo + MLA_QK_PAD] = _rope_half_block(qh[:, LANE:], c, s).astype(BF16)
    hkv = _rms(z_ref[:, Q_LORA:Q_LORA + KV_LORA].astype(F32), gkv_ref[...]).astype(BF16)
    step = 4 * (MLA_NOPE + MLA_V)
    for lo in range(0, MLA_HEADS * (MLA_NOPE + MLA_V), step):
        kv_ref[:, lo:lo + step] = jnp.dot(hkv, wkv_ref[:, lo:lo + step], preferred_element_type=F32).astype(BF16)
    kr_ref[...] = _rope_half_block(z_ref[:, Q_LORA + KV_LORA:].astype(F32), c, s).astype(BF16)


def _mla_prep(zm, gq, gkv, wq, wkv, c_r, s_r, l, tm=256):
    n = zm.shape[0]
    tm = min(tm, n)
    wq_w = MLA_HEADS * MLA_QK_PAD
    wkv_w = MLA_HEADS * (MLA_NOPE + MLA_V)
    row = lambda w: pl.BlockSpec((tm, w), lambda i: (i, 0))
    full = lambda r, w: pl.BlockSpec((r, w), lambda i: (0, 0))
    layer = lambda r, w: pl.BlockSpec((None, r, w), lambda i: (l, 0, 0))
    return pl.pallas_call(
        _mla_prep_kernel,
        grid=(n // tm,),
        in_specs=[row(MLA_GROUP_W), full(1, Q_LORA), full(1, KV_LORA), layer(Q_LORA, wq_w), layer(KV_LORA, wkv_w),
                  row(LANE), row(LANE)],
        out_specs=[row(wq_w), row(wkv_w), row(LANE)],
        out_shape=[jax.ShapeDtypeStruct((n, wq_w), BF16), jax.ShapeDtypeStruct((n, wkv_w), BF16),
                   jax.ShapeDtypeStruct((n, LANE), BF16)],
        compiler_params=_params("parallel"),
        name="mla_prep",
    )(zm, gq, gkv, wq, wkv, c_r, s_r)


def _attn_kernel(q_ref, kv_ref, kr_ref, o_ref, *, tq, tk, hp):
    i = pl.program_id(2)
    c = (MLA_NOPE + MLA_ROPE) ** -0.5 * np.log2(np.e)
    hw = MLA_NOPE + MLA_V

    def tile(k0, carry, masked):
        kr = kr_ref[pl.ds(k0, tk), :]
        if masked:
            qc = (i * tq + lax.broadcasted_iota(jnp.int32, (tq, tk), 0)) // CHUNK
            kc = (k0 + lax.broadcasted_iota(jnp.int32, (tq, tk), 1)) // CHUNK
            mask = kc <= qc
        new = []
        for j in range(hp):
            m, l, acc = carry[j]
            q = q_ref[:, j * MLA_QK_PAD:(j + 1) * MLA_QK_PAD]
            k = jnp.concatenate([kv_ref[pl.ds(k0, tk), j * hw:j * hw + MLA_NOPE], kr], axis=1)
            s = lax.dot_general(q, k, (((1,), (1,)), ((), ())), preferred_element_type=F32)
            if masked:
                s = jnp.where(mask, s, -1e30)
            m_new = jnp.maximum(m, jnp.max(s, axis=-1, keepdims=True))
            alpha = jnp.exp2((m - m_new) * c)
            p = jnp.exp2((s - m_new) * c)
            l = alpha * l + jnp.sum(p, axis=-1, keepdims=True)
            v = kv_ref[pl.ds(k0, tk), j * hw + MLA_NOPE:(j + 1) * hw]
            acc = alpha * acc + jnp.dot(p.astype(BF16), v, preferred_element_type=F32)
            new.append((m_new, l, acc))
        return tuple(new)

    def body(kb, carry):
        return tile(pl.multiple_of(kb * tk, tk), carry, False)

    carry = tuple((jnp.full((tq, 1), -1e30, F32), jnp.zeros((tq, 1), F32), jnp.zeros((tq, MLA_V), F32))
                  for _ in range(hp))
    n_full = (i * tq) // tk
    carry = lax.fori_loop(0, n_full, body, carry)
    for d in range(max(1, tq // tk)):
        carry = tile(pl.multiple_of((n_full + d) * tk, tk), carry, True)
    for j in range(hp):
        m, l, acc = carry[j]
        o_ref[:, j * MLA_V:(j + 1) * MLA_V] = (acc / l).astype(BF16)


def _attention(q, kv, kr, b, s, tq=512, tk=1024, hp=4):
    n = q.shape[0]
    tq = min(tq, s)
    tk = min(tk, s)
    nq = s // tq
    kv3 = kv.reshape(b, s, kv.shape[1])
    kr3 = kr.reshape(b, s, LANE)
    return pl.pallas_call(
        functools.partial(_attn_kernel, tq=tq, tk=tk, hp=hp),
        grid=(b, MLA_HEADS // hp, nq),
        in_specs=[
            pl.BlockSpec((tq, hp * MLA_QK_PAD), lambda bi, h, i: (bi * nq + i, h)),
            pl.BlockSpec((None, s, hp * (MLA_NOPE + MLA_V)), lambda bi, h, i: (bi, 0, h)),
            pl.BlockSpec((None, s, LANE), lambda bi, h, i: (bi, 0, 0)),
        ],
        out_specs=pl.BlockSpec((tq, hp * MLA_V), lambda bi, h, i: (bi * nq + i, h)),
        out_shape=jax.ShapeDtypeStruct((n, MLA_HEADS * MLA_V), BF16),
        compiler_params=_params("parallel", "parallel", "arbitrary"),
        name="mla_attention",
    )(q, kv3, kr3)


def _ret_kernel(lg_ref, q_ref, k_ref, v_ref, g_ref, c_ref, s_ref, o_ref, state_ref, dec_ref, *, t, hp):
    first = pl.program_id(2) == 0
    c = c_ref[...]
    s = s_ref[...]
    half = RET_QK // 2
    pos = lax.broadcasted_iota(jnp.int32, (t, 1), 0).astype(F32)

    def rope(x):
        x1 = x[:, :half]
        x2 = x[:, half:]
        return jnp.concatenate([x1 * c - x2 * s, x2 * c + x1 * s], axis=1)

    for j in range(hp):
        lg = lg_ref[pl.program_id(1) * hp + j]
        cols = slice(j * RET_QK, (j + 1) * RET_QK)

        @pl.when(first)
        def _():
            state_ref[j] = jnp.zeros((RET_QK, RET_V), F32)
            ii = lax.broadcasted_iota(jnp.int32, (t, t), 0)
            jj = lax.broadcasted_iota(jnp.int32, (t, t), 1)
            dec_ref[j] = jnp.where(jj // CHUNK <= ii // CHUNK, jnp.exp(jnp.abs(ii - jj).astype(F32) * lg), 0.0)

        q = rope(q_ref[:, cols].astype(F32))
        k = rope(k_ref[:, cols].astype(F32)) * RET_QK ** -0.5
        v = v_ref[:, cols]
        q_dec = q * jnp.exp((pos + 1.0) * lg)
        k_dec = k * jnp.exp((t - 1.0 - pos) * lg)
        a = lax.dot_general(q.astype(BF16), k.astype(BF16), (((1,), (1,)), ((), ())),
                            preferred_element_type=F32) * dec_ref[j]
        state = state_ref[j]
        o = jnp.dot(a.astype(BF16), v, preferred_element_type=F32)
        o = o + jnp.dot(q_dec.astype(BF16), state.astype(BF16), preferred_element_type=F32)
        block_decay = jnp.exp(jnp.full((1, RET_V), t * 1.0, F32) * lg)
        state_ref[j] = state * block_decay + lax.dot_general(
            k_dec.astype(BF16), v, (((0,), (0,)), ((), ())), preferred_element_type=F32)

        mu = jnp.mean(o, axis=-1, keepdims=True)
        d = o - mu
        var = jnp.mean(d * d, axis=-1, keepdims=True)
        on = d * lax.rsqrt(var + RET_GN_EPS)
        g = g_ref[:, cols].astype(F32)
        o_ref[:, cols] = (g * jax.nn.sigmoid(g) * on).astype(BF16)


def _retention(zr, c_k, s_k, b, s, t=512, hp=2):
    n = zr.shape[0]
    t = min(t, s)
    nt = s // t
    ng = RET_HEADS // hp
    log_g = jnp.log1p(-(2.0 ** (-5.0 - jnp.arange(RET_HEADS, dtype=F32))))
    col = lambda part: pl.BlockSpec((t, hp * RET_QK), lambda bi, h, ti: (bi * nt + ti, part * ng + h))
    tab = pl.BlockSpec((t, LANE), lambda bi, h, ti: (bi * nt + ti, 0))
    return pl.pallas_call(
        functools.partial(_ret_kernel, t=t, hp=hp),
        grid=(b, ng, nt),
        in_specs=[pl.BlockSpec(memory_space=pltpu.SMEM), col(0), col(1), col(2), col(3), tab, tab],
        out_specs=pl.BlockSpec((t, hp * RET_V), lambda bi, h, ti: (bi * nt + ti, h)),
        out_shape=jax.ShapeDtypeStruct((n, RET_HEADS * RET_V), BF16),
        scratch_shapes=[pltpu.VMEM((hp, RET_QK, RET_V), F32), pltpu.VMEM((hp, t, t), F32)],
        compiler_params=_params("parallel", "parallel", "arbitrary"),
        name="retention",
    )(log_g, zr, zr, zr, zr, c_k, s_k)


def _gmlp_kernel(u_ref, v_ref, lng_ref, lnb_ref, ws_ref, bst_ref, o_ref, *, nblk):
    v = jax.nn.gelu(v_ref[...].astype(F32))
    mu = jnp.mean(v, axis=-1, keepdims=True)
    d = v - mu
    var = jnp.mean(d * d, axis=-1, keepdims=True)
    vn = (d * lax.rsqrt(var + EPS) * lng_ref[...] + lnb_ref[...]).astype(BF16)
    pc_i = lax.broadcasted_iota(jnp.int32, (GM_BLOCK, GM_BLOCK), 0) // CHUNK
    pc_j = lax.broadcasted_iota(jnp.int32, (GM_BLOCK, GM_BLOCK), 1) // CHUNK
    gw = GM_WIDTH // GM_GROUPS
    for g in range(GM_GROUPS):
        w = jnp.where(pc_i >= pc_j, ws_ref[g], 0.0).astype(BF16)
        bias = bst_ref[:, g:g + 1]
        for r in range(nblk):
            rows = slice(r * GM_BLOCK, (r + 1) * GM_BLOCK)
            cols = slice(g * gw, (g + 1) * gw)
            mixed = jnp.dot(w, vn[rows, cols], preferred_element_type=F32) + bias
            u = jax.nn.gelu(u_ref[rows, cols].astype(F32))
            o_ref[rows, cols] = (u * mixed).astype(BF16)


def _gmlp(zg, ln_g, ln_b, w_s, b_s_t, l, nblk=2):
    n = zg.shape[0]
    tm = nblk * GM_BLOCK
    return pl.pallas_call(
        functools.partial(_gmlp_kernel, nblk=nblk),
        grid=(n // tm,),
        in_specs=[
            pl.BlockSpec((tm, GM_WIDTH), lambda i: (i, 0)),
            pl.BlockSpec((tm, GM_WIDTH), lambda i: (i, 1)),
            pl.BlockSpec((1, GM_WIDTH), lambda i: (0, 0)),
            pl.BlockSpec((1, GM_WIDTH), lambda i: (0, 0)),
            pl.BlockSpec((None, GM_GROUPS, GM_BLOCK, GM_BLOCK), lambda i: (l, 0, 0, 0)),
            pl.BlockSpec((GM_BLOCK, GM_GROUPS), lambda i: (0, 0)),
        ],
        out_specs=pl.BlockSpec((tm, GM_WIDTH), lambda i: (i, 0)),
        out_shape=jax.ShapeDtypeStruct((n, GM_WIDTH), BF16),
        compiler_params=_params("parallel"),
        name="gmlp",
    )(zg, zg, ln_g, ln_b, w_s, b_s_t)


def _merge_kernel(h_ref, ya_ref, yb_ref, yc_ref, wg0_ref, wg1_ref, wg2_ref, bg0_ref, bg1_ref, bg2_ref,
                  wb0_ref, wb1_ref, wb2_ref, o_ref):
    h = h_ref[...]

    def branch(y_ref, wg_ref, bg_ref, wb_ref):
        gate = jax.nn.sigmoid(jnp.dot(h, wg_ref[...], preferred_element_type=F32) + bg_ref[...])
        return gate * jnp.dot(y_ref[...], wb_ref[...], preferred_element_type=F32)

    merged = branch(ya_ref, wg0_ref, bg0_ref, wb0_ref)
    merged = merged + branch(yb_ref, wg1_ref, bg1_ref, wb1_ref)
    merged = merged + branch(yc_ref, wg2_ref, bg2_ref, wb2_ref)
    o_ref[...] = merged.astype(BF16)


def _merge(h, ya, yb, yc, w_gate, b_gate, w_br, l, tm=512, tn=512):
    n, d = h.shape
    tm = min(tm, n)
    nj = d // tn
    act = pl.BlockSpec((tm, d), lambda i, j: (i, 0))
    wg = lambda br: pl.BlockSpec((None, d, tn), lambda i, j: (l, 0, br * nj + j))
    bg = lambda br: pl.BlockSpec((1, tn), lambda i, j: (0, br * nj + j))
    wb = lambda br: pl.BlockSpec((None, None, d, tn), lambda i, j: (l, br, 0, j))
    return pl.pallas_call(
        _merge_kernel,
        grid=(n // tm, nj),
        in_specs=[act, act, act, act, wg(0), wg(1), wg(2), bg(0), bg(1), bg(2), wb(0), wb(1), wb(2)],
        out_specs=pl.BlockSpec((tm, tn), lambda i, j: (i, j)),
        out_shape=jax.ShapeDtypeStruct((n, d), BF16),
        compiler_params=_params("parallel", "arbitrary"),
        name="merge",
    )(h, ya, yb, yc, w_gate, w_gate, w_gate, b_gate, b_gate, b_gate, w_br, w_br, w_br)


def _out_proj_kernel(m_ref, w_ref, g_ref, x_ref, o_ref):
    y = jnp.dot(m_ref[...], w_ref[...], preferred_element_type=F32)
    o_ref[...] = x_ref[...] + _rms(y, g_ref[...])


def _out_proj(merged, w_o, g_post, x, l, tm=512):
    n, d = x.shape
    tm = min(tm, n)
    return pl.pallas_call(
        _out_proj_kernel,
        grid=(n // tm,),
        in_specs=[
            pl.BlockSpec((tm, d), lambda i: (i, 0)),
            pl.BlockSpec((None, d, d), lambda i: (l, 0, 0)),
            pl.BlockSpec((1, d), lambda i: (0, 0)),
            pl.BlockSpec((tm, d), lambda i: (i, 0)),
        ],
        out_specs=pl.BlockSpec((tm, d), lambda i: (i, 0)),
        out_shape=jax.ShapeDtypeStruct((n, d), F32),
        compiler_params=_params("parallel"),
        name="out_proj",
    )(merged, w_o, g_post, x)


def _rot_cols(w):
    half = w.shape[-1] // 2
    return jnp.concatenate([-w[..., half:], w[..., :half]], axis=-1)


def _cast_wi_kernel(w_ref, o_ref):
    o_ref[:, :D_FF] = w_ref[...].astype(BF16)
    o_ref[:, D_FF:] = jnp.zeros((o_ref.shape[0], D_FF_PAD - D_FF), BF16)


def _cast_wi(wi, tr=256):
    nl, d, _ = wi.shape
    return pl.pallas_call(
        _cast_wi_kernel,
        grid=(nl, 2, d // tr),
        in_specs=[pl.BlockSpec((None, tr, D_FF), lambda l, h, r: (l, r, h))],
        out_specs=pl.BlockSpec((None, None, tr, D_FF_PAD), lambda l, h, r: (l, h, r, 0)),
        out_shape=jax.ShapeDtypeStruct((nl, 2, d, D_FF_PAD), BF16),
        compiler_params=_params("parallel", "parallel", "parallel"),
        name="cast_wi",
    )(wi)


def _cast_wo_kernel(w_ref, o_ref):
    rows = pl.program_id(1) * FF_TILE + lax.broadcasted_iota(jnp.int32, (FF_TILE, 1), 0)
    o_ref[...] = jnp.where(rows < D_FF, w_ref[...], 0.0).astype(BF16)


def _cast_wo(wo):
    nl, _, d = wo.shape
    spec = pl.BlockSpec((None, FF_TILE, d), lambda l, j: (l, j, 0))
    return pl.pallas_call(
        _cast_wo_kernel,
        grid=(nl, D_FF_PAD // FF_TILE),
        in_specs=[spec],
        out_specs=spec,
        out_shape=jax.ShapeDtypeStruct((nl, D_FF_PAD, d), BF16),
        compiler_params=_params("parallel", "parallel"),
        name="cast_wo",
    )(wo)


W_IN_SHIFT = MLA_ROPE
W_IN_TILE = 1024


def _cast_w_in_kernel(a_ref, b_ref, o_ref):
    o_ref[...] = jnp.concatenate([a_ref[:, W_IN_SHIFT:], b_ref[:, :W_IN_SHIFT]], axis=1).astype(BF16)


def _cast_w_in(w_in, tr=512):
    nl, d, _ = w_in.shape
    first = (Q_LORA + KV_LORA) // W_IN_TILE
    per_tile = W_IN_TILE // LANE
    return pl.pallas_call(
        _cast_w_in_kernel,
        grid=(nl, d // tr, (RET_GROUP_W + GM_GROUP_W) // W_IN_TILE),
        in_specs=[pl.BlockSpec((None, tr, W_IN_TILE), lambda l, r, j: (l, r, first + j)),
                  pl.BlockSpec((None, tr, LANE), lambda l, r, j: (l, r, (first + j + 1) * per_tile))],
        out_specs=pl.BlockSpec((None, tr, W_IN_TILE), lambda l, r, j: (l, r, j)),
        out_shape=jax.ShapeDtypeStruct((nl, d, RET_GROUP_W + GM_GROUP_W), BF16),
        compiler_params=_params("parallel", "parallel", "parallel"),
        name="cast_w_in",
    )(w_in, w_in)


def _cast_w_mla_kernel(a_ref, b_ref, o_ref):
    o_kr = Q_LORA + KV_LORA
    o_ref[:, :o_kr] = a_ref[...].astype(BF16)
    w_kr = b_ref[:, :MLA_ROPE]
    o_ref[:, o_kr:] = jnp.concatenate([w_kr, _rot_cols(w_kr)], axis=1).astype(BF16)


def _cast_w_mla(w_in, tr=512):
    nl, d, _ = w_in.shape
    o_kr = Q_LORA + KV_LORA
    return pl.pallas_call(
        _cast_w_mla_kernel,
        grid=(nl, d // tr),
        in_specs=[pl.BlockSpec((None, tr, o_kr), lambda l, r: (l, r, 0)),
                  pl.BlockSpec((None, tr, LANE), lambda l, r: (l, r, o_kr // LANE))],
        out_specs=pl.BlockSpec((None, tr, MLA_GROUP_W), lambda l, r: (l, r, 0)),
        out_shape=jax.ShapeDtypeStruct((nl, d, MLA_GROUP_W), BF16),
        compiler_params=_params("parallel", "parallel"),
        name="cast_w_mla",
    )(w_in, w_in)


def _prep_w_uq(w_uq):
    nl = w_uq.shape[0]
    w = w_uq.reshape(nl, Q_LORA, MLA_HEADS, MLA_NOPE + MLA_ROPE)
    w_rope = w[..., MLA_NOPE:]
    w = jnp.concatenate([w, _rot_cols(w_rope)], axis=-1)
    return w.reshape(nl, Q_LORA, MLA_HEADS * MLA_QK_PAD).astype(BF16)


def _prep_weights(p):
    w = {}
    w["ffn1"] = (_cast_wi(p["ffn1_wi"]), _cast_wo(p["ffn1_wo"]))
    w["ffn2"] = (_cast_wi(p["ffn2_wi"]), _cast_wo(p["ffn2_wo"]))
    w["w_mla"] = _cast_w_mla(p["w_in"])
    w["w_rg"] = _cast_w_in(p["w_in"])
    w["w_uq"] = _prep_w_uq(p["w_uq"])
    for name in ("w_ukv", "w_gate", "w_br", "w_o"):
        w[name] = p[name].astype(BF16)
    w["gm_b_s_t"] = jnp.swapaxes(p["gm_b_s"], 1, 2)
    return w


def _row(v):
    return v.reshape(1, -1)


def _token_mixer(x, l, p, w, tables, b, s):
    c_r, s_r, c_k, s_k = tables
    zm, h = _norm_mm(x, _row(p["mix_pre_g"][l]), w["w_mla"], l)
    zr = _mm(h, w["w_rg"], l, 0, RET_GROUP_W)
    zg = _mm(h, w["w_rg"], l, RET_GROUP_W, GM_GROUP_W)
    q, kv, kr = _mla_prep(zm, _row(p["q_norm_g"][l]), _row(p["kv_norm_g"][l]), w["w_uq"], w["w_ukv"], c_r, s_r, l)
    y_a = _attention(q, kv, kr, b, s)
    y_b = _retention(zr, c_k, s_k, b, s)
    y_c = _gmlp(zg, _row(p["gm_ln_g"][l]), _row(p["gm_ln_b"][l]), p["gm_w_s"], w["gm_b_s_t"][l], l)
    merged = _merge(h, y_a, y_b, y_c, w["w_gate"], _row(p["b_gate"][l]), w["w_br"], l)
    return _out_proj(merged, w["w_o"], _row(p["mix_post_g"][l]), x, l)


def _trunk(x, pos, p, depth):
    b, s, d = x.shape
    tables = _rope_tables(pos)
    w = _prep_weights(p)
    x = x.reshape(b * s, d)
    for l in range(depth):
        x = _ffn(x, _row(p["ffn1_pre_g"][l]), *w["ffn1"], _row(p["ffn1_post_g"][l]), l)
        x = _token_mixer(x, l, p, w, tables, b, s)
        x = _ffn(x, _row(p["ffn2_pre_g"][l]), *w["ffn2"], _row(p["ffn2_post_g"][l]), l)
    return x.reshape(b, s, d)


def kernel(x, pos, ffn1_pre_g, ffn1_wi, ffn1_wo, ffn1_post_g, mix_pre_g, w_in, q_norm_g, w_uq, kv_norm_g, w_ukv, gm_ln_g, gm_ln_b, gm_w_s, gm_b_s, w_gate, b_gate, w_br, w_o, mix_post_g, ffn2_pre_g, ffn2_wi, ffn2_wo, ffn2_post_g):
    p = dict(ffn1_pre_g=ffn1_pre_g, ffn1_wi=ffn1_wi, ffn1_wo=ffn1_wo, ffn1_post_g=ffn1_post_g, mix_pre_g=mix_pre_g,
             w_in=w_in, q_norm_g=q_norm_g, w_uq=w_uq, kv_norm_g=kv_norm_g, w_ukv=w_ukv, gm_ln_g=gm_ln_g,
             gm_ln_b=gm_ln_b, gm_w_s=gm_w_s, gm_b_s=gm_b_s, w_gate=w_gate, b_gate=b_gate, w_br=w_br, w_o=w_o,
             mix_post_g=mix_post_g, ffn2_pre_g=ffn2_pre_g, ffn2_wi=ffn2_wi, ffn2_wo=ffn2_wo, ffn2_post_g=ffn2_post_g)
    return _trunk(x, pos, p, DEPTH)
```

```python
import functools

import numpy as np
import jax
import jax.numpy as jnp
from jax import lax
from jax.experimental import pallas as pl
from jax.experimental.pallas import tpu as pltpu

F32 = jnp.float32
BF16 = jnp.bfloat16

D_MODEL = 2048
DEPTH = 4
CHUNK = 64
EPS = 1e-6
ROPE_BASE = 10000.0
MLA_HEADS = 16
MLA_NOPE = 128
MLA_ROPE = 64
MLA_V = 128
Q_LORA = 512
KV_LORA = 512
RET_HEADS = 8
RET_QK = 256
RET_V = 256
RET_GN_EPS = 1e-5
GM_GROUPS = 4
GM_WIDTH = 2048
GM_BLOCK = 128
N_BRANCH = 3
D_FF = 5504

LANE = 128
MLA_QK_PAD = 2 * LANE
MLA_GROUP_W = Q_LORA + KV_LORA + LANE
RET_GROUP_W = 4 * RET_HEADS * RET_QK
GM_GROUP_W = 2 * GM_WIDTH
FF_TILE = 512
D_FF_PAD = ((D_FF + FF_TILE - 1) // FF_TILE) * FF_TILE
VMEM_LIMIT = 56 * 2 ** 20


def _params(*sem, flags=None):
    return pltpu.CompilerParams(dimension_semantics=sem, vmem_limit_bytes=VMEM_LIMIT, flags=flags)


def _rms(x, g):
    return x * lax.rsqrt(jnp.mean(x * x, axis=-1, keepdims=True) + EPS) * g


def _rope_table_kernel(ang_r_ref, ang_k_ref, cr_ref, sr_ref, ck_ref, sk_ref):
    a = ang_r_ref[...]
    live = lax.broadcasted_iota(jnp.int32, a.shape, 1) < MLA_ROPE
    cr_ref[...] = jnp.where(live, jnp.cos(a), 0.0)
    sr_ref[...] = jnp.where(live, jnp.sin(a), 0.0)
    k = ang_k_ref[...]
    ck_ref[...] = jnp.cos(k)
    sk_ref[...] = jnp.sin(k)


def _rope_tables(pos):
    n = pos.size
    p = pos.astype(F32).reshape(n, 1)
    inv_r = ROPE_BASE ** (-jnp.arange(0, MLA_ROPE, 2, dtype=F32) / MLA_ROPE)
    inv_k = ROPE_BASE ** (-jnp.arange(0, RET_QK, 2, dtype=F32) / RET_QK)
    ang_r = p * inv_r
    ang_r = jnp.concatenate([ang_r, ang_r, jnp.zeros((n, LANE - MLA_ROPE), F32)], axis=1)
    ang_k = p * inv_k
    tm = min(n, 1024)
    spec = pl.BlockSpec((tm, LANE), lambda i: (i, 0))
    out = jax.ShapeDtypeStruct((n, LANE), F32)
    return pl.pallas_call(
        _rope_table_kernel,
        grid=(n // tm,),
        in_specs=[spec, spec],
        out_specs=[spec] * 4,
        out_shape=[out] * 4,
        compiler_params=_params("parallel"),
        name="rope_tables",
    )(ang_r, ang_k)


def _ffn_kernel(x_ref, gpre_ref, wa_ref, wb_ref, wo_ref, gpost_ref, o_ref, h_ref):
    f = pl.program_id(1)

    def hidden_tile(h):
        a = jnp.dot(h, wa_ref[...], preferred_element_type=F32)
        b = jnp.dot(h, wb_ref[...], preferred_element_type=F32)
        act = (a * jax.nn.sigmoid(a) * b).astype(BF16)
        return jnp.dot(act, wo_ref[...], preferred_element_type=F32)

    @pl.when(f == 0)
    def _():
        h = _rms(x_ref[...], gpre_ref[...]).astype(BF16)
        h_ref[...] = h
        o_ref[...] = hidden_tile(h)

    @pl.when(f > 0)
    def _():
        o_ref[...] += hidden_tile(h_ref[...])

    @pl.when(f == pl.num_programs(1) - 1)
    def _():
        o_ref[...] = x_ref[...] + _rms(o_ref[...], 0.5 * gpost_ref[...])


def _ffn(x, g_pre, wi, wo, g_post, l, tm=1024):
    n, d = x.shape
    nf = D_FF_PAD // FF_TILE
    tm = min(tm, n)
    return pl.pallas_call(
        _ffn_kernel,
        grid=(n // tm, nf),
        in_specs=[
            pl.BlockSpec((tm, d), lambda i, f: (i, 0)),
            pl.BlockSpec((1, d), lambda i, f: (0, 0)),
            pl.BlockSpec((None, None, d, FF_TILE), lambda i, f: (l, 0, 0, f)),
            pl.BlockSpec((None, None, d, FF_TILE), lambda i, f: (l, 1, 0, f)),
            pl.BlockSpec((None, FF_TILE, d), lambda i, f: (l, f, 0)),
            pl.BlockSpec((1, d), lambda i, f: (0, 0)),
        ],
        out_specs=pl.BlockSpec((tm, d), lambda i, f: (i, 0)),
        out_shape=jax.ShapeDtypeStruct((n, d), F32),
        scratch_shapes=[pltpu.VMEM((tm, d), BF16)],
        compiler_params=_params("parallel", "arbitrary"),
        name="ffn",
    )(x, g_pre, wi, wi, wo, g_post)


def _norm_mm_kernel(x_ref, g_ref, w_ref, o_ref, h_ref):
    h = _rms(x_ref[...], g_ref[...]).astype(BF16)
    h_ref[...] = h
    o_ref[...] = jnp.dot(h, w_ref[...], preferred_element_type=F32).astype(o_ref.dtype)


def _norm_mm(x, g, w, l, tm=1024):
    n, k = x.shape
    nc = w.shape[2]
    tm = min(tm, n)
    return pl.pallas_call(
        _norm_mm_kernel,
        grid=(n // tm,),
        in_specs=[pl.BlockSpec((tm, k), lambda i: (i, 0)), pl.BlockSpec((1, k), lambda i: (0, 0)),
                  pl.BlockSpec((None, k, nc), lambda i: (l, 0, 0))],
        out_specs=[pl.BlockSpec((tm, nc), lambda i: (i, 0)), pl.BlockSpec((tm, k), lambda i: (i, 0))],
        out_shape=[jax.ShapeDtypeStruct((n, nc), BF16), jax.ShapeDtypeStruct((n, k), BF16)],
        compiler_params=_params("parallel"),
        name="norm_in_proj",
    )(x, g, w)


def _mm_kernel(a_ref, w_ref, o_ref):
    o_ref[...] = jnp.dot(a_ref[...], w_ref[...], preferred_element_type=F32).astype(o_ref.dtype)


def _mm(a, w, l, col0, nc, tn=1024, tm=2048):
    n, k = a.shape
    j0 = col0 // tn
    tm = min(tm, n)
    return pl.pallas_call(
        _mm_kernel,
        grid=(n // tm, nc // tn),
        in_specs=[pl.BlockSpec((tm, k), lambda i, j: (i, 0)), pl.BlockSpec((None, k, tn), lambda i, j: (l, 0, j0 + j))],
        out_specs=pl.BlockSpec((tm, tn), lambda i, j: (i, j)),
        out_shape=jax.ShapeDtypeStruct((n, nc), BF16),
        compiler_params=_params("parallel", "arbitrary"),
        name="in_proj",
    )(a, w)


def _rope_half_block(blk, c, s):
    return blk * c + pltpu.roll(blk, MLA_ROPE, axis=1) * s


def _mla_prep_kernel(z_ref, gq_ref, gkv_ref, wq_ref, wkv_ref, c_ref, s_ref, q_ref, kv_ref, kr_ref):
    c = c_ref[...]
    s = s_ref[...]
    hq = _rms(z_ref[:, :Q_LORA].astype(F32), gq_ref[...]).astype(BF16)
    for h in range(MLA_HEADS):
        lo = h * MLA_QK_PAD
        qh = jnp.dot(hq, wq_ref[:, lo:lo + MLA_QK_PAD], preferred_element_type=F32)
        q_ref[:, lo:lo + LANE] = qh[:, :LANE].astype(BF16)
        q_ref[:, lo + LANE:lo + MLA_QK_PAD] = _rope_half_block(qh[:, LANE:], c, s).astype(BF16)
    hkv = _rms(z_ref[:, Q_LORA:Q_LORA + KV_LORA].astype(F32), gkv_ref[...]).astype(BF16)
    step = 4 * (MLA_NOPE + MLA_V)
    for lo in range(0, MLA_HEADS * (MLA_NOPE + MLA_V), step):
        kv_ref[:, lo:lo + step] = jnp.dot(hkv, wkv_ref[:, lo:lo + step], preferred_element_type=F32).astype(BF16)
    kr_ref[...] = _rope_half_block(z_ref[:, Q_LORA + KV_LORA:].astype(F32), c, s).astype(BF16)


def _mla_prep(zm, gq, gkv, wq, wkv, c_r, s_r, l, tm=256):
    n = zm.shape[0]
    tm = min(tm, n)
    wq_w = MLA_HEADS * MLA_QK_PAD
    wkv_w = MLA_HEADS * (MLA_NOPE + MLA_V)
    row = lambda w: pl.BlockSpec((tm, w), lambda i: (i, 0))
    full = lambda r, w: pl.BlockSpec((r, w), lambda i: (0, 0))
    layer = lambda r, w: pl.BlockSpec((None, r, w), lambda i: (l, 0, 0))
    return pl.pallas_call(
        _mla_prep_kernel,
        grid=(n // tm,),
        in_specs=[row(MLA_GROUP_W), full(1, Q_LORA), full(1, KV_LORA), layer(Q_LORA, wq_w), layer(KV_LORA, wkv_w),
                  row(LANE), row(LANE)],
        out_specs=[row(wq_w), row(wkv_w), row(LANE)],
        out_shape=[jax.ShapeDtypeStruct((n, wq_w), BF16), jax.ShapeDtypeStruct((n, wkv_w), BF16),
                   jax.ShapeDtypeStruct((n, LANE), BF16)],
        compiler_params=_params("parallel"),
        name="mla_prep",
    )(zm, gq, gkv, wq, wkv, c_r, s_r)


def _attn_kernel(q_ref, kv_ref, kr_ref, o_ref, *, tq, tk, hp):
    i = pl.program_id(2)
    c = (MLA_NOPE + MLA_ROPE) ** -0.5 * np.log2(np.e)
    hw = MLA_NOPE + MLA_V

    def tile(k0, carry, masked):
        kr = kr_ref[pl.ds(k0, tk), :]
        if masked:
            qc = (i * tq + lax.broadcasted_iota(jnp.int32, (tq, tk), 0)) // CHUNK
            kc = (k0 + lax.broadcasted_iota(jnp.int32, (tq, tk), 1)) // CHUNK
            mask = kc <= qc
        new = []
        for j in range(hp):
            m, l, acc = carry[j]
            q = q_ref[:, j * MLA_QK_PAD:(j + 1) * MLA_QK_PAD]
            k = jnp.concatenate([kv_ref[pl.ds(k0, tk), j * hw:j * hw + MLA_NOPE], kr], axis=1)
            s = lax.dot_general(q, k, (((1,), (1,)), ((), ())), preferred_element_type=F32)
            if masked:
                s = jnp.where(mask, s, -1e30)
            m_new = jnp.maximum(m, jnp.max(s, axis=-1, keepdims=True))
            alpha = jnp.exp2((m - m_new) * c)
            p = jnp.exp2((s - m_new) * c)
            l = alpha * l + jnp.sum(p, axis=-1, keepdims=True)
            v = kv_ref[pl.ds(k0, tk), j * hw + MLA_NOPE:(j + 1) * hw]
            acc = alpha * acc + jnp.dot(p.astype(BF16), v, preferred_element_type=F32)
            new.append((m_new, l, acc))
        return tuple(new)

    def body(kb, carry):
        return tile(pl.multiple_of(kb * tk, tk), carry, False)

    carry = tuple((jnp.full((tq, 1), -1e30, F32), jnp.zeros((tq, 1), F32), jnp.zeros((tq, MLA_V), F32))
                  for _ in range(hp))
    n_full = (i * tq) // tk
    carry = lax.fori_loop(0, n_full, body, carry)
    for d in range(max(1, tq // tk)):
        carry = tile(pl.multiple_of((n_full + d) * tk, tk), carry, True)
    for j in range(hp):
        m, l, acc = carry[j]
        o_ref[:, j * MLA_V:(j + 1) * MLA_V] = (acc / l).astype(BF16)


def _attention(q, kv, kr, b, s, tq=512, tk=1024, hp=4):
    n = q.shape[0]
    tq = min(tq, s)
    tk = min(tk, s)
    nq = s // tq
    kv3 = kv.reshape(b, s, kv.shape[1])
    kr3 = kr.reshape(b, s, LANE)
    return pl.pallas_call(
        functools.partial(_attn_kernel, tq=tq, tk=tk, hp=hp),
        grid=(b, MLA_HEADS // hp, nq),
        in_specs=[
            pl.BlockSpec((tq, hp * MLA_QK_PAD), lambda bi, h, i: (bi * nq + i, h)),
            pl.BlockSpec((None, s, hp * (MLA_NOPE + MLA_V)), lambda bi, h, i: (bi, 0, h)),
            pl.BlockSpec((None, s, LANE), lambda bi, h, i: (bi, 0, 0)),
        ],
        out_specs=pl.BlockSpec((tq, hp * MLA_V), lambda bi, h, i: (bi * nq + i, h)),
        out_shape=jax.ShapeDtypeStruct((n, MLA_HEADS * MLA_V), BF16),
        compiler_params=_params("parallel", "parallel", "arbitrary"),
        name="mla_attention",
    )(q, kv3, kr3)


def _ret_kernel(lg_ref, q_ref, k_ref, v_ref, g_ref, c_ref, s_ref, o_ref, state_ref, dec_ref, *, t, hp):
    first = pl.program_id(2) == 0
    c = c_ref[...]
    s = s_ref[...]
    half = RET_QK // 2
    pos = lax.broadcasted_iota(jnp.int32, (t, 1), 0).astype(F32)

    def rope(x):
        x1 = x[:, :half]
        x2 = x[:, half:]
        return jnp.concatenate([x1 * c - x2 * s, x2 * c + x1 * s], axis=1)

    for j in range(hp):
        lg = lg_ref[pl.program_id(1) * hp + j]
        cols = slice(j * RET_QK, (j + 1) * RET_QK)

        @pl.when(first)
        def _():
            state_ref[j] = jnp.zeros((RET_QK, RET_V), F32)
            ii = lax.broadcasted_iota(jnp.int32, (t, t), 0)
            jj = lax.broadcasted_iota(jnp.int32, (t, t), 1)
            dec_ref[j] = jnp.where(jj // CHUNK <= ii // CHUNK, jnp.exp(jnp.abs(ii - jj).astype(F32) * lg), 0.0)

        q = rope(q_ref[:, cols].astype(F32))
        k = rope(k_ref[:, cols].astype(F32)) * RET_QK ** -0.5
        v = v_ref[:, cols]
        q_dec = q * jnp.exp((pos + 1.0) * lg)
        k_dec = k * jnp.exp((t - 1.0 - pos) * lg)
        a = lax.dot_general(q.astype(BF16), k.astype(BF16), (((1,), (1,)), ((), ())),
                            preferred_element_type=F32) * dec_ref[j]
        state = state_ref[j]
        o = jnp.dot(a.astype(BF16), v, preferred_element_type=F32)
        o = o + jnp.dot(q_dec.astype(BF16), state.astype(BF16), preferred_element_type=F32)
        block_decay = jnp.exp(jnp.full((1, RET_V), t * 1.0, F32) * lg)
        state_ref[j] = state * block_decay + lax.dot_general(
            k_dec.astype(BF16), v, (((0,), (0,)), ((), ())), preferred_element_type=F32)

        mu = jnp.mean(o, axis=-1, keepdims=True)
        d = o - mu
        var = jnp.mean(d * d, axis=-1, keepdims=True)
        on = d * lax.rsqrt(var + RET_GN_EPS)
        g = g_ref[:, cols].astype(F32)
        o_ref[:, cols] = (g * jax.nn.sigmoid(g) * on).astype(BF16)


def _retention(zr, c_k, s_k, b, s, t=512, hp=2):
    n = zr.shape[0]
    t = min(t, s)
    nt = s // t
    ng = RET_HEADS // hp
    log_g = jnp.log1p(-(2.0 ** (-5.0 - jnp.arange(RET_HEADS, dtype=F32))))
    col = lambda part: pl.BlockSpec((t, hp * RET_QK), lambda bi, h, ti: (bi * nt + ti, part * ng + h))
    tab = pl.BlockSpec((t, LANE), lambda bi, h, ti: (bi * nt + ti, 0))
    return pl.pallas_call(
        functools.partial(_ret_kernel, t=t, hp=hp),
        grid=(b, ng, nt),
        in_specs=[pl.BlockSpec(memory_space=pltpu.SMEM), col(0), col(1), col(2), col(3), tab, tab],
        out_specs=pl.BlockSpec((t, hp * RET_V), lambda bi, h, ti: (bi * nt + ti, h)),
        out_shape=jax.ShapeDtypeStruct((n, RET_HEADS * RET_V), BF16),
        scratch_shapes=[pltpu.VMEM((hp, RET_QK, RET_V), F32), pltpu.VMEM((hp, t, t), F32)],
        compiler_params=_params("parallel", "parallel", "arbitrary"),
        name="retention",
    )(log_g, zr, zr, zr, zr, c_k, s_k)


def _gmlp_kernel(u_ref, v_ref, lng_ref, lnb_ref, ws_ref, bst_ref, o_ref, *, nblk):
    v = jax.nn.gelu(v_ref[...].astype(F32))
    mu = jnp.mean(v, axis=-1, keepdims=True)
    d = v - mu
    var = jnp.mean(d * d, axis=-1, keepdims=True)
    vn = (d * lax.rsqrt(var + EPS) * lng_ref[...] + lnb_ref[...]).astype(BF16)
    pc_i = lax.broadcasted_iota(jnp.int32, (GM_BLOCK, GM_BLOCK), 0) // CHUNK
    pc_j = lax.broadcasted_iota(jnp.int32, (GM_BLOCK, GM_BLOCK), 1) // CHUNK
    gw = GM_WIDTH // GM_GROUPS
    for g in range(GM_GROUPS):
        w = jnp.where(pc_i >= pc_j, ws_ref[g], 0.0).astype(BF16)
        bias = bst_ref[:, g:g + 1]
        for r in range(nblk):
            rows = slice(r * GM_BLOCK, (r + 1) * GM_BLOCK)
            cols = slice(g * gw, (g + 1) * gw)
            mixed = jnp.dot(w, vn[rows, cols], preferred_element_type=F32) + bias
            u = jax.nn.gelu(u_ref[rows, cols].astype(F32))
            o_ref[rows, cols] = (u * mixed).astype(BF16)


def _gmlp(zg, ln_g, ln_b, w_s, b_s_t, l, nblk=2):
    n = zg.shape[0]
    tm = nblk * GM_BLOCK
    return pl.pallas_call(
        functools.partial(_gmlp_kernel, nblk=nblk),
        grid=(n // tm,),
        in_specs=[
            pl.BlockSpec((tm, GM_WIDTH), lambda i: (i, 0)),
            pl.BlockSpec((tm, GM_WIDTH), lambda i: (i, 1)),
            pl.BlockSpec((1, GM_WIDTH), lambda i: (0, 0)),
            pl.BlockSpec((1, GM_WIDTH), lambda i: (0, 0)),
            pl.BlockSpec((None, GM_GROUPS, GM_BLOCK, GM_BLOCK), lambda i: (l, 0, 0, 0)),
            pl.BlockSpec((GM_BLOCK, GM_GROUPS), lambda i: (0, 0)),
        ],
        out_specs=pl.BlockSpec((tm, GM_WIDTH), lambda i: (i, 0)),
        out_shape=jax.ShapeDtypeStruct((n, GM_WIDTH), BF16),
        compiler_params=_params("parallel"),
        name="gmlp",
    )(zg, zg, ln_g, ln_b, w_s, b_s_t)


def _merge_kernel(h_ref, ya_ref, yb_ref, yc_ref, wg0_ref, wg1_ref, wg2_ref, bg0_ref, bg1_ref, bg2_ref,
                  wb0_ref, wb1_ref, wb2_ref, o_ref):
    h = h_ref[...]

    def branch(y_ref, wg_ref, bg_ref, wb_ref):
        gate = jax.nn.sigmoid(jnp.dot(h, wg_ref[...], preferred_element_type=F32) + bg_ref[...])
        return gate * jnp.dot(y_ref[...], wb_ref[...], preferred_element_type=F32)

    merged = branch(ya_ref, wg0_ref, bg0_ref, wb0_ref)
    merged = merged + branch(yb_ref, wg1_ref, bg1_ref, wb1_ref)
    merged = merged + branch(yc_ref, wg2_ref, bg2_ref, wb2_ref)
    o_ref[...] = merged.astype(BF16)


def _merge(h, ya, yb, yc, w_gate, b_gate, w_br, l, tm=512, tn=512):
    n, d = h.shape
    tm = min(tm, n)
    nj = d // tn
    act = pl.BlockSpec((tm, d), lambda i, j: (i, 0))
    wg = lambda br: pl.BlockSpec((None, d, tn), lambda i, j: (l, 0, br * nj + j))
    bg = lambda br: pl.BlockSpec((1, tn), lambda i, j: (0, br * nj + j))
    wb = lambda br: pl.BlockSpec((None, None, d, tn), lambda i, j: (l, br, 0, j))
    return pl.pallas_call(
        _merge_kernel,
        grid=(n // tm, nj),
        in_specs=[act, act, act, act, wg(0), wg(1), wg(2), bg(0), bg(1), bg(2), wb(0), wb(1), wb(2)],
        out_specs=pl.BlockSpec((tm, tn), lambda i, j: (i, j)),
        out_shape=jax.ShapeDtypeStruct((n, d), BF16),
        compiler_params=_params("parallel", "arbitrary"),
        name="merge",
    )(h, ya, yb, yc, w_gate, w_gate, w_gate, b_gate, b_gate, b_gate, w_br, w_br, w_br)


def _out_proj_kernel(m_ref, w_ref, g_ref, x_ref, o_ref):
    y = jnp.dot(m_ref[...], w_ref[...], preferred_element_type=F32)
    o_ref[...] = x_ref[...] + _rms(y, g_ref[...])


def _out_proj(merged, w_o, g_post, x, l, tm=512):
    n, d = x.shape
    tm = min(tm, n)
    return pl.pallas_call(
        _out_proj_kernel,
        grid=(n // tm,),
        in_specs=[
            pl.BlockSpec((tm, d), lambda i: (i, 0)),
            pl.BlockSpec((None, d, d), lambda i: (l, 0, 0)),
            pl.BlockSpec((1, d), lambda i: (0, 0)),
            pl.BlockSpec((tm, d), lambda i: (i, 0)),
        ],
        out_specs=pl.BlockSpec((tm, d), lambda i: (i, 0)),
        out_shape=jax.ShapeDtypeStruct((n, d), F32),
        compiler_params=_params("parallel"),
        name="out_proj",
    )(merged, w_o, g_post, x)


def _rot_cols(w):
    half = w.shape[-1] // 2
    return jnp.concatenate([-w[..., half:], w[..., :half]], axis=-1)


def _cast_wi_kernel(w_ref, o_ref):
    o_ref[:, :D_FF] = w_ref[...].astype(BF16)
    o_ref[:, D_FF:] = jnp.zeros((o_ref.shape[0], D_FF_PAD - D_FF), BF16)


def _cast_wi(wi, tr=256):
    nl, d, _ = wi.shape
    return pl.pallas_call(
        _cast_wi_kernel,
        grid=(nl, 2, d // tr),
        in_specs=[pl.BlockSpec((None, tr, D_FF), lambda l, h, r: (l, r, h))],
        out_specs=pl.BlockSpec((None, None, tr, D_FF_PAD), lambda l, h, r: (l, h, r, 0)),
        out_shape=jax.ShapeDtypeStruct((nl, 2, d, D_FF_PAD), BF16),
        compiler_params=_params("parallel", "parallel", "parallel"),
        name="cast_wi",
    )(wi)


def _cast_wo_kernel(w_ref, o_ref):
    rows = pl.program_id(1) * FF_TILE + lax.broadcasted_iota(jnp.int32, (FF_TILE, 1), 0)
    o_ref[...] = jnp.where(rows < D_FF, w_ref[...], 0.0).astype(BF16)


def _cast_wo(wo):
    nl, _, d = wo.shape
    spec = pl.BlockSpec((None, FF_TILE, d), lambda l, j: (l, j, 0))
    return pl.pallas_call(
        _cast_wo_kernel,
        grid=(nl, D_FF_PAD // FF_TILE),
        in_specs=[spec],
        out_specs=spec,
        out_shape=jax.ShapeDtypeStruct((nl, D_FF_PAD, d), BF16),
        compiler_params=_params("parallel", "parallel"),
        name="cast_wo",
    )(wo)


W_IN_SHIFT = MLA_ROPE
W_IN_TILE = 1024


def _cast_w_in_kernel(a_ref, b_ref, o_ref):
    o_ref[...] = jnp.concatenate([a_ref[:, W_IN_SHIFT:], b_ref[:, :W_IN_SHIFT]], axis=1).astype(BF16)


def _cast_w_in(w_in, tr=512):
    nl, d, _ = w_in.shape
    first = (Q_LORA + KV_LORA) // W_IN_TILE
    per_tile = W_IN_TILE // LANE
    return pl.pallas_call(
        _cast_w_in_kernel,
        grid=(nl, d // tr, (RET_GROUP_W + GM_GROUP_W) // W_IN_TILE),
        in_specs=[pl.BlockSpec((None, tr, W_IN_TILE), lambda l, r, j: (l, r, first + j)),
                  pl.BlockSpec((None, tr, LANE), lambda l, r, j: (l, r, (first + j + 1) * per_tile))],
        out_specs=pl.BlockSpec((None, tr, W_IN_TILE), lambda l, r, j: (l, r, j)),
        out_shape=jax.ShapeDtypeStruct((nl, d, RET_GROUP_W + GM_GROUP_W), BF16),
        compiler_params=_params("parallel", "parallel", "parallel"),
        name="cast_w_in",
    )(w_in, w_in)


def _cast_w_mla_kernel(a_ref, b_ref, o_ref):
    o_kr = Q_LORA + KV_LORA
    o_ref[:, :o_kr] = a_ref[...].astype(BF16)
    w_kr = b_ref[:, :MLA_ROPE]
    o_ref[:, o_kr:] = jnp.concatenate([w_kr, _rot_cols(w_kr)], axis=1).astype(BF16)


def _cast_w_mla(w_in, tr=512):
    nl, d, _ = w_in.shape
    o_kr = Q_LORA + KV_LORA
    return pl.pallas_call(
        _cast_w_mla_kernel,
        grid=(nl, d // tr),
        in_specs=[pl.BlockSpec((None, tr, o_kr), lambda l, r: (l, r, 0)),
                  pl.BlockSpec((None, tr, LANE), lambda l, r: (l, r, o_kr // LANE))],
        out_specs=pl.BlockSpec((None, tr, MLA_GROUP_W), lambda l, r: (l, r, 0)),
        out_shape=jax.ShapeDtypeStruct((nl, d, MLA_GROUP_W), BF16),
        compiler_params=_params("parallel", "parallel"),
        name="cast_w_mla",
    )(w_in, w_in)


def _prep_w_uq(w_uq):
    nl = w_uq.shape[0]
    w = w_uq.reshape(nl, Q_LORA, MLA_HEADS, MLA_NOPE + MLA_ROPE)
    w_rope = w[..., MLA_NOPE:]
    w = jnp.concatenate([w, _rot_cols(w_rope)], axis=-1)
    return w.reshape(nl, Q_LORA, MLA_HEADS * MLA_QK_PAD).astype(BF16)


def _prep_weights(p):
    w = {}
    w["ffn1"] = (_cast_wi(p["ffn1_wi"]), _cast_wo(p["ffn1_wo"]))
    w["ffn2"] = (_cast_wi(p["ffn2_wi"]), _cast_wo(p["ffn2_wo"]))
    w_in = p["w_in"].astype(BF16)
    w["w_mla"] = _cast_w_mla(w_in)
    w["w_rg"] = _cast_w_in(w_in)
    w["w_uq"] = _prep_w_uq(p["w_uq"])
    for name in ("w_ukv", "w_gate", "w_br", "w_o"):
        w[name] = p[name].astype(BF16)
    w["gm_b_s_t"] = jnp.swapaxes(p["gm_b_s"], 1, 2)
    return w


def _row(v):
    return v.reshape(1, -1)


def _token_mixer(x, l, p, w, tables, b, s):
    c_r, s_r, c_k, s_k = tables
    zm, h = _norm_mm(x, _row(p["mix_pre_g"][l]), w["w_mla"], l)
    zr = _mm(h, w["w_rg"], l, 0, RET_GROUP_W)
    zg = _mm(h, w["w_rg"], l, RET_GROUP_W, GM_GROUP_W)
    q, kv, kr = _mla_prep(zm, _row(p["q_norm_g"][l]), _row(p["kv_norm_g"][l]), w["w_uq"], w["w_ukv"], c_r, s_r, l)
    y_a = _attention(q, kv, kr, b, s)
    y_b = _retention(zr, c_k, s_k, b, s)
    y_c = _gmlp(zg, _row(p["gm_ln_g"][l]), _row(p["gm_ln_b"][l]), p["gm_w_s"], w["gm_b_s_t"][l], l)
    merged = _merge(h, y_a, y_b, y_c, w["w_gate"], _row(p["b_gate"][l]), w["w_br"], l)
    return _out_proj(merged, w["w_o"], _row(p["mix_post_g"][l]), x, l)


def _trunk(x, pos, p, depth):
    b, s, d = x.shape
    tables = _rope_tables(pos)
    w = _prep_weights(p)
    x = x.reshape(b * s, d)
    for l in range(depth):
        x = _ffn(x, _row(p["ffn1_pre_g"][l]), *w["ffn1"], _row(p["ffn1_post_g"][l]), l)
        x = _token_mixer(x, l, p, w, tables, b, s)
        x = _ffn(x, _row(p["ffn2_pre_g"][l]), *w["ffn2"], _row(p["ffn2_post_g"][l]), l)
    return x.reshape(b, s, d)


def kernel(x, pos, ffn1_pre_g, ffn1_wi, ffn1_wo, ffn1_post_g, mix_pre_g, w_in, q_norm_g, w_uq, kv_norm_g, w_ukv, gm_ln_g, gm_ln_b, gm_w_s, gm_b_s, w_gate, b_gate, w_br, w_o, mix_post_g, ffn2_pre_g, ffn2_wi, ffn2_wo, ffn2_post_g):
    p = dict(ffn1_pre_g=ffn1_pre_g, ffn1_wi=ffn1_wi, ffn1_wo=ffn1_wo, ffn1_post_g=ffn1_post_g, mix_pre_g=mix_pre_g,
             w_in=w_in, q_norm_g=q_norm_g, w_uq=w_uq, kv_norm_g=kv_norm_g, w_ukv=w_ukv, gm_ln_g=gm_ln_g,
             gm_ln_b=gm_ln_b, gm_w_s=gm_w_s, gm_b_s=gm_b_s, w_gate=w_gate, b_gate=b_gate, w_br=w_br, w_o=w_o,
             mix_post_g=mix_post_g, ffn2_pre_g=ffn2_pre_g, ffn2_wi=ffn2_wi, ffn2_wo=ffn2_wo, ffn2_post_g=ffn2_post_g)
    return _trunk(x, pos, p, DEPTH)
```

```python
import functools

import numpy as np
import jax
import jax.numpy as jnp
from jax import lax
from jax.experimental import pallas as pl
from jax.experimental.pallas import tpu as pltpu

F32 = jnp.float32
BF16 = jnp.bfloat16

D_MODEL = 2048
DEPTH = 4
CHUNK = 64
EPS = 1e-6
ROPE_BASE = 10000.0
MLA_HEADS = 16
MLA_NOPE = 128
MLA_ROPE = 64
MLA_V = 128
Q_LORA = 512
KV_LORA = 512
RET_HEADS = 8
RET_QK = 256
RET_V = 256
RET_GN_EPS = 1e-5
GM_GROUPS = 4
GM_WIDTH = 2048
GM_BLOCK = 128
N_BRANCH = 3
D_FF = 5504

LANE = 128
MLA_QK_PAD = 2 * LANE
MLA_GROUP_W = Q_LORA + KV_LORA + LANE
RET_GROUP_W = 4 * RET_HEADS * RET_QK
GM_GROUP_W = 2 * GM_WIDTH
FF_TILE = 512
D_FF_PAD = ((D_FF + FF_TILE - 1) // FF_TILE) * FF_TILE
VMEM_LIMIT = 56 * 2 ** 20


def _params(*sem, flags=None):
    return pltpu.CompilerParams(dimension_semantics=sem, vmem_limit_bytes=VMEM_LIMIT, flags=flags)


def _rms(x, g):
    return x * lax.rsqrt(jnp.mean(x * x, axis=-1, keepdims=True) + EPS) * g


def _rope_table_kernel(ang_r_ref, ang_k_ref, cr_ref, sr_ref, ck_ref, sk_ref):
    a = ang_r_ref[...]
    live = lax.broadcasted_iota(jnp.int32, a.shape, 1) < MLA_ROPE
    cr_ref[...] = jnp.where(live, jnp.cos(a), 0.0)
    sr_ref[...] = jnp.where(live, jnp.sin(a), 0.0)
    k = ang_k_ref[...]
    ck_ref[...] = jnp.cos(k)
    sk_ref[...] = jnp.sin(k)


def _rope_tables(pos):
    n = pos.size
    p = pos.astype(F32).reshape(n, 1)
    inv_r = ROPE_BASE ** (-jnp.arange(0, MLA_ROPE, 2, dtype=F32) / MLA_ROPE)
    inv_k = ROPE_BASE ** (-jnp.arange(0, RET_QK, 2, dtype=F32) / RET_QK)
    ang_r = p * inv_r
    ang_r = jnp.concatenate([ang_r, ang_r, jnp.zeros((n, LANE - MLA_ROPE), F32)], axis=1)
    ang_k = p * inv_k
    tm = min(n, 1024)
    spec = pl.BlockSpec((tm, LANE), lambda i: (i, 0))
    out = jax.ShapeDtypeStruct((n, LANE), F32)
    return pl.pallas_call(
        _rope_table_kernel,
        grid=(n // tm,),
        in_specs=[spec, spec],
        out_specs=[spec] * 4,
        out_shape=[out] * 4,
        compiler_params=_params("parallel"),
        name="rope_tables",
    )(ang_r, ang_k)


def _ffn_kernel(x_ref, gpre_ref, wa_ref, wb_ref, wo_ref, gpost_ref, o_ref, h_ref):
    f = pl.program_id(1)

    def hidden_tile(h):
        a = jnp.dot(h, wa_ref[...], preferred_element_type=F32)
        b = jnp.dot(h, wb_ref[...], preferred_element_type=F32)
        act = (a * jax.nn.sigmoid(a) * b).astype(BF16)
        return jnp.dot(act, wo_ref[...], preferred_element_type=F32)

    @pl.when(f == 0)
    def _():
        h = _rms(x_ref[...], gpre_ref[...]).astype(BF16)
        h_ref[...] = h
        o_ref[...] = hidden_tile(h)

    @pl.when(f > 0)
    def _():
        o_ref[...] += hidden_tile(h_ref[...])

    @pl.when(f == pl.num_programs(1) - 1)
    def _():
        o_ref[...] = x_ref[...] + _rms(o_ref[...], 0.5 * gpost_ref[...])


def _ffn(x, g_pre, wi, wo, g_post, l, tm=1024):
    n, d = x.shape
    nf = D_FF_PAD // FF_TILE
    tm = min(tm, n)
    return pl.pallas_call(
        _ffn_kernel,
        grid=(n // tm, nf),
        in_specs=[
            pl.BlockSpec((tm, d), lambda i, f: (i, 0)),
            pl.BlockSpec((1, d), lambda i, f: (0, 0)),
            pl.BlockSpec((None, None, d, FF_TILE), lambda i, f: (l, 0, 0, f)),
            pl.BlockSpec((None, None, d, FF_TILE), lambda i, f: (l, 1, 0, f)),
            pl.BlockSpec((None, FF_TILE, d), lambda i, f: (l, f, 0)),
            pl.BlockSpec((1, d), lambda i, f: (0, 0)),
        ],
        out_specs=pl.BlockSpec((tm, d), lambda i, f: (i, 0)),
        out_shape=jax.ShapeDtypeStruct((n, d), F32),
        scratch_shapes=[pltpu.VMEM((tm, d), BF16)],
        compiler_params=_params("parallel", "arbitrary"),
        name="ffn",
    )(x, g_pre, wi, wi, wo, g_post)


def _norm_mm_kernel(x_ref, g_ref, w_ref, o_ref, h_ref):
    h = _rms(x_ref[...], g_ref[...]).astype(BF16)
    h_ref[...] = h
    o_ref[...] = jnp.dot(h, w_ref[...], preferred_element_type=F32).astype(o_ref.dtype)


def _norm_mm(x, g, w, l, tm=1024):
    n, k = x.shape
    nc = w.shape[2]
    tm = min(tm, n)
    return pl.pallas_call(
        _norm_mm_kernel,
        grid=(n // tm,),
        in_specs=[pl.BlockSpec((tm, k), lambda i: (i, 0)), pl.BlockSpec((1, k), lambda i: (0, 0)),
                  pl.BlockSpec((None, k, nc), lambda i: (l, 0, 0))],
        out_specs=[pl.BlockSpec((tm, nc), lambda i: (i, 0)), pl.BlockSpec((tm, k), lambda i: (i, 0))],
        out_shape=[jax.ShapeDtypeStruct((n, nc), BF16), jax.ShapeDtypeStruct((n, k), BF16)],
        compiler_params=_params("parallel"),
        name="norm_in_proj",
    )(x, g, w)


def _mm_kernel(a_ref, w_ref, o_ref):
    o_ref[...] = jnp.dot(a_ref[...], w_ref[...], preferred_element_type=F32).astype(o_ref.dtype)


def _mm(a, w, l, col0, nc, tn=1024, tm=2048):
    n, k = a.shape
    j0 = col0 // tn
    tm = min(tm, n)
    return pl.pallas_call(
        _mm_kernel,
        grid=(n // tm, nc // tn),
        in_specs=[pl.BlockSpec((tm, k), lambda i, j: (i, 0)), pl.BlockSpec((None, k, tn), lambda i, j: (l, 0, j0 + j))],
        out_specs=pl.BlockSpec((tm, tn), lambda i, j: (i, j)),
        out_shape=jax.ShapeDtypeStruct((n, nc), BF16),
        compiler_params=_params("parallel", "arbitrary"),
        name="in_proj",
    )(a, w)


def _rope_half_block(blk, c, s):
    return blk * c + pltpu.roll(blk, MLA_ROPE, axis=1) * s


def _mla_prep_kernel(z_ref, gq_ref, gkv_ref, wq_ref, wkv_ref, c_ref, s_ref, q_ref, kv_ref, kr_ref):
    c = c_ref[...]
    s = s_ref[...]
    hq = _rms(z_ref[:, :Q_LORA].astype(F32), gq_ref[...]).astype(BF16)
    for h in range(MLA_HEADS):
        lo = h * MLA_QK_PAD
        qh = jnp.dot(hq, wq_ref[:, lo:lo + MLA_QK_PAD], preferred_element_type=F32)
        q_ref[:, lo:lo + LANE] = qh[:, :LANE].astype(BF16)
        q_ref[:, lo + LANE:lo + MLA_QK_PAD] = _rope_half_block(qh[:, LANE:], c, s).astype(BF16)
    hkv = _rms(z_ref[:, Q_LORA:Q_LORA + KV_LORA].astype(F32), gkv_ref[...]).astype(BF16)
    step = 4 * (MLA_NOPE + MLA_V)
    for lo in range(0, MLA_HEADS * (MLA_NOPE + MLA_V), step):
        kv_ref[:, lo:lo + step] = jnp.dot(hkv, wkv_ref[:, lo:lo + step], preferred_element_type=F32).astype(BF16)
    kr_ref[...] = _rope_half_block(z_ref[:, Q_LORA + KV_LORA:].astype(F32), c, s).astype(BF16)


def _mla_prep(zm, gq, gkv, wq, wkv, c_r, s_r, l, tm=512):
    n = zm.shape[0]
    tm = min(tm, n)
    wq_w = MLA_HEADS * MLA_QK_PAD
    wkv_w = MLA_HEADS * (MLA_NOPE + MLA_V)
    row = lambda w: pl.BlockSpec((tm, w), lambda i: (i, 0))
    full = lambda r, w: pl.BlockSpec((r, w), lambda i: (0, 0))
    layer = lambda r, w: pl.BlockSpec((None, r, w), lambda i: (l, 0, 0))
    return pl.pallas_call(
        _mla_prep_kernel,
        grid=(n // tm,),
        in_specs=[row(MLA_GROUP_W), full(1, Q_LORA), full(1, KV_LORA), layer(Q_LORA, wq_w), layer(KV_LORA, wkv_w),
                  row(LANE), row(LANE)],
        out_specs=[row(wq_w), row(wkv_w), row(LANE)],
        out_shape=[jax.ShapeDtypeStruct((n, wq_w), BF16), jax.ShapeDtypeStruct((n, wkv_w), BF16),
                   jax.ShapeDtypeStruct((n, LANE), BF16)],
        compiler_params=_params("parallel"),
        name="mla_prep",
    )(zm, gq, gkv, wq, wkv, c_r, s_r)


def _attn_kernel(q_ref, kv_ref, kr_ref, o_ref, *, tq, tk, hp):
    i = pl.program_id(2)
    c = (MLA_NOPE + MLA_ROPE) ** -0.5 * np.log2(np.e)
    hw = MLA_NOPE + MLA_V

    def tile(k0, carry, width, masked_from):
        kr = kr_ref[pl.ds(k0, width), :]
        if masked_from is not None:
            mw = width - masked_from
            qc = lax.broadcasted_iota(jnp.int32, (tq, mw), 0) // CHUNK
            kc = lax.broadcasted_iota(jnp.int32, (tq, mw), 1) // CHUNK
            mask = kc <= qc
        new = []
        for j in range(hp):
            m, l, acc = carry[j]
            q = q_ref[:, j * MLA_QK_PAD:(j + 1) * MLA_QK_PAD]
            k = jnp.concatenate([kv_ref[pl.ds(k0, width), j * hw:j * hw + MLA_NOPE], kr], axis=1)
            s = lax.dot_general(q, k, (((1,), (1,)), ((), ())), preferred_element_type=F32)
            if masked_from is not None:
                tail = jnp.where(mask, s[:, masked_from:], -1e30)
                s = tail if masked_from == 0 else jnp.concatenate([s[:, :masked_from], tail], axis=1)
            m_new = jnp.maximum(m, jnp.max(s, axis=-1, keepdims=True))
            alpha = jnp.exp2((m - m_new) * c)
            p = jnp.exp2((s - m_new) * c)
            l = alpha * l + jnp.sum(p, axis=-1, keepdims=True)
            v = kv_ref[pl.ds(k0, width), j * hw + MLA_NOPE:(j + 1) * hw]
            acc = alpha * acc + jnp.dot(p.astype(BF16), v, preferred_element_type=F32)
            new.append((m_new, l, acc))
        return tuple(new)

    def body(kb, carry):
        return tile(pl.multiple_of(kb * tk, tk), carry, tk, None)

    def finish(carry):
        for j in range(hp):
            m, l, acc = carry[j]
            o_ref[:, j * MLA_V:(j + 1) * MLA_V] = (acc / l).astype(BF16)

    carry = tuple((jnp.full((tq, 1), -1e30, F32), jnp.zeros((tq, 1), F32), jnp.zeros((tq, MLA_V), F32))
                  for _ in range(hp))
    q0 = i * tq
    n_full = q0 // tk
    carry = lax.fori_loop(0, n_full, body, carry)
    for lead in range(0, tk, tq):
        @pl.when(q0 - n_full * tk == lead)
        def _():
            finish(tile(pl.multiple_of(q0 - lead, tq), carry, lead + tq, lead))


def _attention(q, kv, kr, b, s, tq=512, tk=1024, hp=4):
    n = q.shape[0]
    tq = min(tq, s)
    tk = min(tk, s)
    assert tk % tq == 0
    nq = s // tq
    kv3 = kv.reshape(b, s, kv.shape[1])
    kr3 = kr.reshape(b, s, LANE)
    return pl.pallas_call(
        functools.partial(_attn_kernel, tq=tq, tk=tk, hp=hp),
        grid=(b, MLA_HEADS // hp, nq),
        in_specs=[
            pl.BlockSpec((tq, hp * MLA_QK_PAD), lambda bi, h, i: (bi * nq + i, h)),
            pl.BlockSpec((None, s, hp * (MLA_NOPE + MLA_V)), lambda bi, h, i: (bi, 0, h)),
            pl.BlockSpec((None, s, LANE), lambda bi, h, i: (bi, 0, 0)),
        ],
        out_specs=pl.BlockSpec((tq, hp * MLA_V), lambda bi, h, i: (bi * nq + i, h)),
        out_shape=jax.ShapeDtypeStruct((n, MLA_HEADS * MLA_V), BF16),
        compiler_params=_params("parallel", "parallel", "arbitrary"),
        name="mla_attention",
    )(q, kv3, kr3)


def _ret_kernel(lg_ref, q_ref, k_ref, v_ref, g_ref, c_ref, s_ref, o_ref, state_ref, dec_ref, *, t, hp):
    first = pl.program_id(2) == 0
    c = c_ref[...]
    s = s_ref[...]
    half = RET_QK // 2
    pos = lax.broadcasted_iota(jnp.int32, (t, 1), 0).astype(F32)

    def rope(x):
        x1 = x[:, :half]
        x2 = x[:, half:]
        return jnp.concatenate([x1 * c - x2 * s, x2 * c + x1 * s], axis=1)

    for j in range(hp):
        lg = lg_ref[pl.program_id(1) * hp + j]
        cols = slice(j * RET_QK, (j + 1) * RET_QK)

        @pl.when(first)
        def _():
            state_ref[j] = jnp.zeros((RET_QK, RET_V), F32)
            ii = lax.broadcasted_iota(jnp.int32, (t, t), 0)
            jj = lax.broadcasted_iota(jnp.int32, (t, t), 1)
            dec_ref[j] = jnp.where(jj // CHUNK <= ii // CHUNK, jnp.exp(jnp.abs(ii - jj).astype(F32) * lg), 0.0)

        q = rope(q_ref[:, cols].astype(F32))
        k = rope(k_ref[:, cols].astype(F32)) * RET_QK ** -0.5
        v = v_ref[:, cols]
        q_dec = q * jnp.exp((pos + 1.0) * lg)
        k_dec = k * jnp.exp((t - 1.0 - pos) * lg)
        a = lax.dot_general(q.astype(BF16), k.astype(BF16), (((1,), (1,)), ((), ())),
                            preferred_element_type=F32) * dec_ref[j]
        state = state_ref[j]
        o = jnp.dot(a.astype(BF16), v, preferred_element_type=F32)
        o = o + jnp.dot(q_dec.astype(BF16), state.astype(BF16), preferred_element_type=F32)
        block_decay = jnp.exp(jnp.full((1, RET_V), t * 1.0, F32) * lg)
        state_ref[j] = state * block_decay + lax.dot_general(
            k_dec.astype(BF16), v, (((0,), (0,)), ((), ())), preferred_element_type=F32)

        mu = jnp.mean(o, axis=-1, keepdims=True)
        d = o - mu
        var = jnp.mean(d * d, axis=-1, keepdims=True)
        on = d * lax.rsqrt(var + RET_GN_EPS)
        g = g_ref[:, cols].astype(F32)
        o_ref[:, cols] = (g * jax.nn.sigmoid(g) * on).astype(BF16)


def _retention(zr, c_k, s_k, b, s, t=512, hp=2):
    n = zr.shape[0]
    t = min(t, s)
    nt = s // t
    ng = RET_HEADS // hp
    log_g = jnp.log1p(-(2.0 ** (-5.0 - jnp.arange(RET_HEADS, dtype=F32))))
    col = lambda part: pl.BlockSpec((t, hp * RET_QK), lambda bi, h, ti: (bi * nt + ti, part * ng + h))
    tab = pl.BlockSpec((t, LANE), lambda bi, h, ti: (bi * nt + ti, 0))
    return pl.pallas_call(
        functools.partial(_ret_kernel, t=t, hp=hp),
        grid=(b, ng, nt),
        in_specs=[pl.BlockSpec(memory_space=pltpu.SMEM), col(0), col(1), col(2), col(3), tab, tab],
        out_specs=pl.BlockSpec((t, hp * RET_V), lambda bi, h, ti: (bi * nt + ti, h)),
        out_shape=jax.ShapeDtypeStruct((n, RET_HEADS * RET_V), BF16),
        scratch_shapes=[pltpu.VMEM((hp, RET_QK, RET_V), F32), pltpu.VMEM((hp, t, t), F32)],
        compiler_params=_params("parallel", "parallel", "arbitrary"),
        name="retention",
    )(log_g, zr, zr, zr, zr, c_k, s_k)


def _gmlp_kernel(u_ref, v_ref, lng_ref, lnb_ref, ws_ref, bst_ref, o_ref, *, nblk):
    v = jax.nn.gelu(v_ref[...].astype(F32))
    mu = jnp.mean(v, axis=-1, keepdims=True)
    d = v - mu
    var = jnp.mean(d * d, axis=-1, keepdims=True)
    vn = (d * lax.rsqrt(var + EPS) * lng_ref[...] + lnb_ref[...]).astype(BF16)
    pc_i = lax.broadcasted_iota(jnp.int32, (GM_BLOCK, GM_BLOCK), 0) // CHUNK
    pc_j = lax.broadcasted_iota(jnp.int32, (GM_BLOCK, GM_BLOCK), 1) // CHUNK
    gw = GM_WIDTH // GM_GROUPS
    for g in range(GM_GROUPS):
        w = jnp.where(pc_i >= pc_j, ws_ref[g], 0.0).astype(BF16)
        bias = bst_ref[:, g:g + 1]
        for r in range(nblk):
            rows = slice(r * GM_BLOCK, (r + 1) * GM_BLOCK)
            cols = slice(g * gw, (g + 1) * gw)
            mixed = jnp.dot(w, vn[rows, cols], preferred_element_type=F32) + bias
            u = jax.nn.gelu(u_ref[rows, cols].astype(F32))
            o_ref[rows, cols] = (u * mixed).astype(BF16)


def _gmlp(zg, ln_g, ln_b, w_s, b_s_t, l, nblk=2):
    n = zg.shape[0]
    tm = nblk * GM_BLOCK
    return pl.pallas_call(
        functools.partial(_gmlp_kernel, nblk=nblk),
        grid=(n // tm,),
        in_specs=[
            pl.BlockSpec((tm, GM_WIDTH), lambda i: (i, 0)),
            pl.BlockSpec((tm, GM_WIDTH), lambda i: (i, 1)),
            pl.BlockSpec((1, GM_WIDTH), lambda i: (0, 0)),
            pl.BlockSpec((1, GM_WIDTH), lambda i: (0, 0)),
            pl.BlockSpec((None, GM_GROUPS, GM_BLOCK, GM_BLOCK), lambda i: (l, 0, 0, 0)),
            pl.BlockSpec((GM_BLOCK, GM_GROUPS), lambda i: (0, 0)),
        ],
        out_specs=pl.BlockSpec((tm, GM_WIDTH), lambda i: (i, 0)),
        out_shape=jax.ShapeDtypeStruct((n, GM_WIDTH), BF16),
        compiler_params=_params("parallel"),
        name="gmlp",
    )(zg, zg, ln_g, ln_b, w_s, b_s_t)


def _merge_kernel(h_ref, ya_ref, yb_ref, yc_ref, wg0_ref, wg1_ref, wg2_ref, bg0_ref, bg1_ref, bg2_ref,
                  wb0_ref, wb1_ref, wb2_ref, o_ref):
    h = h_ref[...]

    def branch(y_ref, wg_ref, bg_ref, wb_ref):
        gate = jax.nn.sigmoid(jnp.dot(h, wg_ref[...], preferred_element_type=F32) + bg_ref[...])
        return gate * jnp.dot(y_ref[...], wb_ref[...], preferred_element_type=F32)

    merged = branch(ya_ref, wg0_ref, bg0_ref, wb0_ref)
    merged = merged + branch(yb_ref, wg1_ref, bg1_ref, wb1_ref)
    merged = merged + branch(yc_ref, wg2_ref, bg2_ref, wb2_ref)
    o_ref[...] = merged.astype(BF16)


def _merge(h, ya, yb, yc, w_gate, b_gate, w_br, l, tm=512, tn=512):
    n, d = h.shape
    tm = min(tm, n)
    nj = d // tn
    act = pl.BlockSpec((tm, d), lambda i, j: (i, 0))
    wg = lambda br: pl.BlockSpec((None, d, tn), lambda i, j: (l, 0, br * nj + j))
    bg = lambda br: pl.BlockSpec((1, tn), lambda i, j: (0, br * nj + j))
    wb = lambda br: pl.BlockSpec((None, None, d, tn), lambda i, j: (l, br, 0, j))
    return pl.pallas_call(
        _merge_kernel,
        grid=(n // tm, nj),
        in_specs=[act, act, act, act, wg(0), wg(1), wg(2), bg(0), bg(1), bg(2), wb(0), wb(1), wb(2)],
        out_specs=pl.BlockSpec((tm, tn), lambda i, j: (i, j)),
        out_shape=jax.ShapeDtypeStruct((n, d), BF16),
        compiler_params=_params("parallel", "arbitrary"),
        name="merge",
    )(h, ya, yb, yc, w_gate, w_gate, w_gate, b_gate, b_gate, b_gate, w_br, w_br, w_br)


def _out_proj_kernel(m_ref, w_ref, g_ref, x_ref, o_ref):
    y = jnp.dot(m_ref[...], w_ref[...], preferred_element_type=F32)
    o_ref[...] = x_ref[...] + _rms(y, g_ref[...])


def _out_proj(merged, w_o, g_post, x, l, tm=512):
    n, d = x.shape
    tm = min(tm, n)
    return pl.pallas_call(
        _out_proj_kernel,
        grid=(n // tm,),
        in_specs=[
            pl.BlockSpec((tm, d), lambda i: (i, 0)),
            pl.BlockSpec((None, d, d), lambda i: (l, 0, 0)),
            pl.BlockSpec((1, d), lambda i: (0, 0)),
            pl.BlockSpec((tm, d), lambda i: (i, 0)),
        ],
        out_specs=pl.BlockSpec((tm, d), lambda i: (i, 0)),
        out_shape=jax.ShapeDtypeStruct((n, d), F32),
        compiler_params=_params("parallel"),
        name="out_proj",
    )(merged, w_o, g_post, x)


def _rot_cols(w):
    half = w.shape[-1] // 2
    return jnp.concatenate([-w[..., half:], w[..., :half]], axis=-1)


def _cast_wi_kernel(w_ref, o_ref):
    o_ref[:, :D_FF] = w_ref[...].astype(BF16)
    o_ref[:, D_FF:] = jnp.zeros((o_ref.shape[0], D_FF_PAD - D_FF), BF16)


def _cast_wi(wi, tr=256):
    nl, d, _ = wi.shape
    return pl.pallas_call(
        _cast_wi_kernel,
        grid=(nl, 2, d // tr),
        in_specs=[pl.BlockSpec((None, tr, D_FF), lambda l, h, r: (l, r, h))],
        out_specs=pl.BlockSpec((None, None, tr, D_FF_PAD), lambda l, h, r: (l, h, r, 0)),
        out_shape=jax.ShapeDtypeStruct((nl, 2, d, D_FF_PAD), BF16),
        compiler_params=_params("parallel", "parallel", "parallel"),
        name="cast_wi",
    )(wi)


def _cast_wo_kernel(w_ref, o_ref):
    rows = pl.program_id(1) * FF_TILE + lax.broadcasted_iota(jnp.int32, (FF_TILE, 1), 0)
    o_ref[...] = jnp.where(rows < D_FF, w_ref[...], 0.0).astype(BF16)


def _cast_wo(wo):
    nl, _, d = wo.shape
    spec = pl.BlockSpec((None, FF_TILE, d), lambda l, j: (l, j, 0))
    return pl.pallas_call(
        _cast_wo_kernel,
        grid=(nl, D_FF_PAD // FF_TILE),
        in_specs=[spec],
        out_specs=spec,
        out_shape=jax.ShapeDtypeStruct((nl, D_FF_PAD, d), BF16),
        compiler_params=_params("parallel", "parallel"),
        name="cast_wo",
    )(wo)


W_IN_SHIFT = MLA_ROPE
W_IN_TILE = 1024


def _cast_w_in_kernel(a_ref, b_ref, o_ref):
    o_ref[...] = jnp.concatenate([a_ref[:, W_IN_SHIFT:], b_ref[:, :W_IN_SHIFT]], axis=1).astype(BF16)


def _cast_w_in(w_in, tr=512):
    nl, d, _ = w_in.shape
    first = (Q_LORA + KV_LORA) // W_IN_TILE
    per_tile = W_IN_TILE // LANE
    return pl.pallas_call(
        _cast_w_in_kernel,
        grid=(nl, d // tr, (RET_GROUP_W + GM_GROUP_W) // W_IN_TILE),
        in_specs=[pl.BlockSpec((None, tr, W_IN_TILE), lambda l, r, j: (l, r, first + j)),
                  pl.BlockSpec((None, tr, LANE), lambda l, r, j: (l, r, (first + j + 1) * per_tile))],
        out_specs=pl.BlockSpec((None, tr, W_IN_TILE), lambda l, r, j: (l, r, j)),
        out_shape=jax.ShapeDtypeStruct((nl, d, RET_GROUP_W + GM_GROUP_W), BF16),
        compiler_params=_params("parallel", "parallel", "parallel"),
        name="cast_w_in",
    )(w_in, w_in)


def _cast_w_mla_kernel(a_ref, b_ref, o_ref):
    o_kr = Q_LORA + KV_LORA
    o_ref[:, :o_kr] = a_ref[...].astype(BF16)
    w_kr = b_ref[:, :MLA_ROPE]
    o_ref[:, o_kr:] = jnp.concatenate([w_kr, _rot_cols(w_kr)], axis=1).astype(BF16)


def _cast_w_mla(w_in, tr=512):
    nl, d, _ = w_in.shape
    o_kr = Q_LORA + KV_LORA
    return pl.pallas_call(
        _cast_w_mla_kernel,
        grid=(nl, d // tr),
        in_specs=[pl.BlockSpec((None, tr, o_kr), lambda l, r: (l, r, 0)),
                  pl.BlockSpec((None, tr, LANE), lambda l, r: (l, r, o_kr // LANE))],
        out_specs=pl.BlockSpec((None, tr, MLA_GROUP_W), lambda l, r: (l, r, 0)),
        out_shape=jax.ShapeDtypeStruct((nl, d, MLA_GROUP_W), BF16),
        compiler_params=_params("parallel", "parallel"),
        name="cast_w_mla",
    )(w_in, w_in)


def _prep_w_uq(w_uq):
    nl = w_uq.shape[0]
    w = w_uq.reshape(nl, Q_LORA, MLA_HEADS, MLA_NOPE + MLA_ROPE)
    w_rope = w[..., MLA_NOPE:]
    w = jnp.concatenate([w, _rot_cols(w_rope)], axis=-1)
    return w.reshape(nl, Q_LORA, MLA_HEADS * MLA_QK_PAD).astype(BF16)


def _prep_weights(p):
    w = {}
    w["ffn1"] = (_cast_wi(p["ffn1_wi"]), _cast_wo(p["ffn1_wo"]))
    w["ffn2"] = (_cast_wi(p["ffn2_wi"]), _cast_wo(p["ffn2_wo"]))
    w_in = p["w_in"].astype(BF16)
    w["w_mla"] = _cast_w_mla(w_in)
    w["w_rg"] = _cast_w_in(w_in)
    w["w_uq"] = _prep_w_uq(p["w_uq"])
    for name in ("w_ukv", "w_gate", "w_br", "w_o"):
        w[name] = p[name].astype(BF16)
    w["gm_b_s_t"] = jnp.swapaxes(p["gm_b_s"], 1, 2)
    return w


def _row(v):
    return v.reshape(1, -1)


def _token_mixer(x, l, p, w, tables, b, s):
    c_r, s_r, c_k, s_k = tables
    zm, h = _norm_mm(x, _row(p["mix_pre_g"][l]), w["w_mla"], l)
    zr = _mm(h, w["w_rg"], l, 0, RET_GROUP_W)
    zg = _mm(h, w["w_rg"], l, RET_GROUP_W, GM_GROUP_W)
    q, kv, kr = _mla_prep(zm, _row(p["q_norm_g"][l]), _row(p["kv_norm_g"][l]), w["w_uq"], w["w_ukv"], c_r, s_r, l)
    y_a = _attention(q, kv, kr, b, s)
    y_b = _retention(zr, c_k, s_k, b, s)
    y_c = _gmlp(zg, _row(p["gm_ln_g"][l]), _row(p["gm_ln_b"][l]), p["gm_w_s"], w["gm_b_s_t"][l], l)
    merged = _merge(h, y_a, y_b, y_c, w["w_gate"], _row(p["b_gate"][l]), w["w_br"], l)
    return _out_proj(merged, w["w_o"], _row(p["mix_post_g"][l]), x, l)


def _trunk(x, pos, p, depth):
    b, s, d = x.shape
    tables = _rope_tables(pos)
    w = _prep_weights(p)
    x = x.reshape(b * s, d)
    for l in range(depth):
        x = _ffn(x, _row(p["ffn1_pre_g"][l]), *w["ffn1"], _row(p["ffn1_post_g"][l]), l)
        x = _token_mixer(x, l, p, w, tables, b, s)
        x = _ffn(x, _row(p["ffn2_pre_g"][l]), *w["ffn2"], _row(p["ffn2_post_g"][l]), l)
    return x.reshape(b, s, d)


def kernel(x, pos, ffn1_pre_g, ffn1_wi, ffn1_wo, ffn1_post_g, mix_pre_g, w_in, q_norm_g, w_uq, kv_norm_g, w_ukv, gm_ln_g, gm_ln_b, gm_w_s, gm_b_s, w_gate, b_gate, w_br, w_o, mix_post_g, ffn2_pre_g, ffn2_wi, ffn2_wo, ffn2_post_g):
    p = dict(ffn1_pre_g=ffn1_pre_g, ffn1_wi=ffn1_wi, ffn1_wo=ffn1_wo, ffn1_post_g=ffn1_post_g, mix_pre_g=mix_pre_g,
             w_in=w_in, q_norm_g=q_norm_g, w_uq=w_uq, kv_norm_g=kv_norm_g, w_ukv=w_ukv, gm_ln_g=gm_ln_g,
             gm_ln_b=gm_ln_b, gm_w_s=gm_w_s, gm_b_s=gm_b_s, w_gate=w_gate, b_gate=b_gate, w_br=w_br, w_o=w_o,
             mix_post_g=mix_post_g, ffn2_pre_g=ffn2_pre_g, ffn2_wi=ffn2_wi, ffn2_wo=ffn2_wo, ffn2_post_g=ffn2_post_g)
    return _trunk(x, pos, p, DEPTH)
```

```python
import functools

import numpy as np
import jax
import jax.numpy as jnp
from jax import lax
from jax.experimental import pallas as pl
from jax.experimental.pallas import tpu as pltpu

F32 = jnp.float32
BF16 = jnp.bfloat16

D_MODEL = 2048
DEPTH = 4
CHUNK = 64
EPS = 1e-6
ROPE_BASE = 10000.0
MLA_HEADS = 16
MLA_NOPE = 128
MLA_ROPE = 64
MLA_V = 128
Q_LORA = 512
KV_LORA = 512
RET_HEADS = 8
RET_QK = 256
RET_V = 256
RET_GN_EPS = 1e-5
GM_GROUPS = 4
GM_WIDTH = 2048
GM_BLOCK = 128
N_BRANCH = 3
D_FF = 5504

LANE = 128
MLA_QK_PAD = 2 * LANE
MLA_GROUP_W = Q_LORA + KV_LORA + LANE
RET_GROUP_W = 4 * RET_HEADS * RET_QK
GM_GROUP_W = 2 * GM_WIDTH
FF_TILE = 512
D_FF_PAD = ((D_FF + FF_TILE - 1) // FF_TILE) * FF_TILE
VMEM_LIMIT = 56 * 2 ** 20


def _params(*sem, flags=None):
    return pltpu.CompilerParams(dimension_semantics=sem, vmem_limit_bytes=VMEM_LIMIT, flags=flags)


def _rms(x, g):
    return x * lax.rsqrt(jnp.mean(x * x, axis=-1, keepdims=True) + EPS) * g


def _rope_table_kernel(ang_r_ref, ang_k_ref, cr_ref, sr_ref, ck_ref, sk_ref):
    a = ang_r_ref[...]
    live = lax.broadcasted_iota(jnp.int32, a.shape, 1) < MLA_ROPE
    cr_ref[...] = jnp.where(live, jnp.cos(a), 0.0)
    sr_ref[...] = jnp.where(live, jnp.sin(a), 0.0)
    k = ang_k_ref[...]
    ck_ref[...] = jnp.cos(k)
    sk_ref[...] = jnp.sin(k)


def _rope_tables(pos):
    n = pos.size
    p = pos.astype(F32).reshape(n, 1)
    inv_r = ROPE_BASE ** (-jnp.arange(0, MLA_ROPE, 2, dtype=F32) / MLA_ROPE)
    inv_k = ROPE_BASE ** (-jnp.arange(0, RET_QK, 2, dtype=F32) / RET_QK)
    ang_r = p * inv_r
    ang_r = jnp.concatenate([ang_r, ang_r, jnp.zeros((n, LANE - MLA_ROPE), F32)], axis=1)
    ang_k = p * inv_k
    tm = min(n, 1024)
    spec = pl.BlockSpec((tm, LANE), lambda i: (i, 0))
    out = jax.ShapeDtypeStruct((n, LANE), F32)
    return pl.pallas_call(
        _rope_table_kernel,
        grid=(n // tm,),
        in_specs=[spec, spec],
        out_specs=[spec] * 4,
        out_shape=[out] * 4,
        compiler_params=_params("parallel"),
        name="rope_tables",
    )(ang_r, ang_k)


def _ffn_kernel(x_ref, gpre_ref, wa_ref, wb_ref, wo_ref, gpost_ref, o_ref, h_ref):
    f = pl.program_id(1)

    def hidden_tile(h):
        a = jnp.dot(h, wa_ref[...], preferred_element_type=F32)
        b = jnp.dot(h, wb_ref[...], preferred_element_type=F32)
        act = (a * jax.nn.sigmoid(a) * b).astype(BF16)
        return jnp.dot(act, wo_ref[...], preferred_element_type=F32)

    @pl.when(f == 0)
    def _():
        h = _rms(x_ref[...], gpre_ref[...]).astype(BF16)
        h_ref[...] = h
        o_ref[...] = hidden_tile(h)

    @pl.when(f > 0)
    def _():
        o_ref[...] += hidden_tile(h_ref[...])

    @pl.when(f == pl.num_programs(1) - 1)
    def _():
        o_ref[...] = x_ref[...] + _rms(o_ref[...], 0.5 * gpost_ref[...])


def _ffn(x, g_pre, wi, wo, g_post, l, tm=1024):
    n, d = x.shape
    nf = D_FF_PAD // FF_TILE
    tm = min(tm, n)
    return pl.pallas_call(
        _ffn_kernel,
        grid=(n // tm, nf),
        in_specs=[
            pl.BlockSpec((tm, d), lambda i, f: (i, 0)),
            pl.BlockSpec((1, d), lambda i, f: (0, 0)),
            pl.BlockSpec((None, None, d, FF_TILE), lambda i, f: (l, 0, 0, f)),
            pl.BlockSpec((None, None, d, FF_TILE), lambda i, f: (l, 1, 0, f)),
            pl.BlockSpec((None, FF_TILE, d), lambda i, f: (l, f, 0)),
            pl.BlockSpec((1, d), lambda i, f: (0, 0)),
        ],
        out_specs=pl.BlockSpec((tm, d), lambda i, f: (i, 0)),
        out_shape=jax.ShapeDtypeStruct((n, d), F32),
        scratch_shapes=[pltpu.VMEM((tm, d), BF16)],
        compiler_params=_params("parallel", "arbitrary"),
        name="ffn",
    )(x, g_pre, wi, wi, wo, g_post)


def _norm_mm_kernel(x_ref, g_ref, w_ref, o_ref, h_ref):
    h = _rms(x_ref[...], g_ref[...]).astype(BF16)
    h_ref[...] = h
    o_ref[...] = jnp.dot(h, w_ref[...], preferred_element_type=F32).astype(o_ref.dtype)


def _norm_mm(x, g, w, l, tm=1024):
    n, k = x.shape
    nc = w.shape[2]
    tm = min(tm, n)
    return pl.pallas_call(
        _norm_mm_kernel,
        grid=(n // tm,),
        in_specs=[pl.BlockSpec((tm, k), lambda i: (i, 0)), pl.BlockSpec((1, k), lambda i: (0, 0)),
                  pl.BlockSpec((None, k, nc), lambda i: (l, 0, 0))],
        out_specs=[pl.BlockSpec((tm, nc), lambda i: (i, 0)), pl.BlockSpec((tm, k), lambda i: (i, 0))],
        out_shape=[jax.ShapeDtypeStruct((n, nc), BF16), jax.ShapeDtypeStruct((n, k), BF16)],
        compiler_params=_params("parallel"),
        name="norm_in_proj",
    )(x, g, w)


def _mm_kernel(a_ref, w_ref, o_ref):
    o_ref[...] = jnp.dot(a_ref[...], w_ref[...], preferred_element_type=F32).astype(o_ref.dtype)


def _mm(a, w, l, col0, nc, tn=1024, tm=2048):
    n, k = a.shape
    j0 = col0 // tn
    tm = min(tm, n)
    return pl.pallas_call(
        _mm_kernel,
        grid=(n // tm, nc // tn),
        in_specs=[pl.BlockSpec((tm, k), lambda i, j: (i, 0)), pl.BlockSpec((None, k, tn), lambda i, j: (l, 0, j0 + j))],
        out_specs=pl.BlockSpec((tm, tn), lambda i, j: (i, j)),
        out_shape=jax.ShapeDtypeStruct((n, nc), BF16),
        compiler_params=_params("parallel", "arbitrary"),
        name="in_proj",
    )(a, w)


def _rope_half_block(blk, c, s):
    return blk * c + pltpu.roll(blk, MLA_ROPE, axis=1) * s


def _mla_prep_kernel(z_ref, gq_ref, gkv_ref, wq_ref, wkv_ref, c_ref, s_ref, q_ref, kv_ref, kr_ref):
    c = c_ref[...]
    s = s_ref[...]
    hq = _rms(z_ref[:, :Q_LORA].astype(F32), gq_ref[...]).astype(BF16)
    for h in range(MLA_HEADS):
        lo = h * MLA_QK_PAD
        qh = jnp.dot(hq, wq_ref[:, lo:lo + MLA_QK_PAD], preferred_element_type=F32)
        q_ref[:, lo:lo + LANE] = qh[:, :LANE].astype(BF16)
        q_ref[:, lo + LANE:lo + MLA_QK_PAD] = _rope_half_block(qh[:, LANE:], c, s).astype(BF16)
    hkv = _rms(z_ref[:, Q_LORA:Q_LORA + KV_LORA].astype(F32), gkv_ref[...]).astype(BF16)
    step = 4 * (MLA_NOPE + MLA_V)
    for lo in range(0, MLA_HEADS * (MLA_NOPE + MLA_V), step):
        kv_ref[:, lo:lo + step] = jnp.dot(hkv, wkv_ref[:, lo:lo + step], preferred_element_type=F32).astype(BF16)
    kr_ref[...] = _rope_half_block(z_ref[:, Q_LORA + KV_LORA:].astype(F32), c, s).astype(BF16)


def _mla_prep(zm, gq, gkv, wq, wkv, c_r, s_r, l, tm=512):
    n = zm.shape[0]
    tm = min(tm, n)
    wq_w = MLA_HEADS * MLA_QK_PAD
    wkv_w = MLA_HEADS * (MLA_NOPE + MLA_V)
    row = lambda w: pl.BlockSpec((tm, w), lambda i: (i, 0))
    full = lambda r, w: pl.BlockSpec((r, w), lambda i: (0, 0))
    layer = lambda r, w: pl.BlockSpec((None, r, w), lambda i: (l, 0, 0))
    return pl.pallas_call(
        _mla_prep_kernel,
        grid=(n // tm,),
        in_specs=[row(MLA_GROUP_W), full(1, Q_LORA), full(1, KV_LORA), layer(Q_LORA, wq_w), layer(KV_LORA, wkv_w),
                  row(LANE), row(LANE)],
        out_specs=[row(wq_w), row(wkv_w), row(LANE)],
        out_shape=[jax.ShapeDtypeStruct((n, wq_w), BF16), jax.ShapeDtypeStruct((n, wkv_w), BF16),
                   jax.ShapeDtypeStruct((n, LANE), BF16)],
        compiler_params=_params("parallel"),
        name="mla_prep",
    )(zm, gq, gkv, wq, wkv, c_r, s_r)


def _attn_kernel(q_ref, kv_ref, kr_ref, o_ref, *, tq, tk, hp):
    i = pl.program_id(2)
    c = (MLA_NOPE + MLA_ROPE) ** -0.5 * np.log2(np.e)
    hw = MLA_NOPE + MLA_V

    def tile(k0, carry, width, masked_from):
        kr = kr_ref[pl.ds(k0, width), :]
        if masked_from is not None:
            mw = width - masked_from
            qc = lax.broadcasted_iota(jnp.int32, (tq, mw), 0) // CHUNK
            kc = lax.broadcasted_iota(jnp.int32, (tq, mw), 1) // CHUNK
            mask = kc <= qc
        new = []
        for j in range(hp):
            m, l, acc = carry[j]
            q = q_ref[:, j * MLA_QK_PAD:(j + 1) * MLA_QK_PAD]
            k = jnp.concatenate([kv_ref[pl.ds(k0, width), j * hw:j * hw + MLA_NOPE], kr], axis=1)
            s = lax.dot_general(q, k, (((1,), (1,)), ((), ())), preferred_element_type=F32)
            if masked_from is not None:
                tail = jnp.where(mask, s[:, masked_from:], -1e30)
                s = tail if masked_from == 0 else jnp.concatenate([s[:, :masked_from], tail], axis=1)
            m_new = jnp.maximum(m, jnp.max(s, axis=-1, keepdims=True))
            alpha = jnp.exp2((m - m_new) * c)
            p = jnp.exp2((s - m_new) * c)
            l = alpha * l + jnp.sum(p, axis=-1, keepdims=True)
            v = kv_ref[pl.ds(k0, width), j * hw + MLA_NOPE:(j + 1) * hw]
            acc = alpha * acc + jnp.dot(p.astype(BF16), v, preferred_element_type=F32)
            new.append((m_new, l, acc))
        return tuple(new)

    def body(kb, carry):
        return tile(pl.multiple_of(kb * tk, tk), carry, tk, None)

    def finish(carry):
        for j in range(hp):
            m, l, acc = carry[j]
            o_ref[:, j * MLA_V:(j + 1) * MLA_V] = (acc / l).astype(BF16)

    carry = tuple((jnp.full((tq, 1), -1e30, F32), jnp.zeros((tq, 1), F32), jnp.zeros((tq, MLA_V), F32))
                  for _ in range(hp))
    q0 = i * tq
    n_full = q0 // tk
    carry = lax.fori_loop(0, n_full, body, carry)
    for lead in range(0, tk, tq):
        @pl.when(q0 - n_full * tk == lead)
        def _():
            finish(tile(pl.multiple_of(q0 - lead, tq), carry, lead + tq, lead))


def _attention(q, kv, kr, b, s, tq=512, tk=2048, hp=4):
    n = q.shape[0]
    tq = min(tq, s)
    tk = min(tk, s)
    assert tk % tq == 0
    nq = s // tq
    kv3 = kv.reshape(b, s, kv.shape[1])
    kr3 = kr.reshape(b, s, LANE)
    return pl.pallas_call(
        functools.partial(_attn_kernel, tq=tq, tk=tk, hp=hp),
        grid=(b, MLA_HEADS // hp, nq),
        in_specs=[
            pl.BlockSpec((tq, hp * MLA_QK_PAD), lambda bi, h, i: (bi * nq + i, h)),
            pl.BlockSpec((None, s, hp * (MLA_NOPE + MLA_V)), lambda bi, h, i: (bi, 0, h)),
            pl.BlockSpec((None, s, LANE), lambda bi, h, i: (bi, 0, 0)),
        ],
        out_specs=pl.BlockSpec((tq, hp * MLA_V), lambda bi, h, i: (bi * nq + i, h)),
        out_shape=jax.ShapeDtypeStruct((n, MLA_HEADS * MLA_V), BF16),
        compiler_params=_params("parallel", "parallel", "arbitrary"),
        name="mla_attention",
    )(q, kv3, kr3)


def _ret_kernel(lg_ref, q_ref, k_ref, v_ref, g_ref, c_ref, s_ref, o_ref, state_ref, dec_ref, *, t, hp):
    first = pl.program_id(2) == 0
    c = c_ref[...]
    s = s_ref[...]
    half = RET_QK // 2
    pos = lax.broadcasted_iota(jnp.int32, (t, 1), 0).astype(F32)

    def rope(x):
        x1 = x[:, :half]
        x2 = x[:, half:]
        return jnp.concatenate([x1 * c - x2 * s, x2 * c + x1 * s], axis=1)

    for j in range(hp):
        lg = lg_ref[pl.program_id(1) * hp + j]
        cols = slice(j * RET_QK, (j + 1) * RET_QK)

        @pl.when(first)
        def _():
            state_ref[j] = jnp.zeros((RET_QK, RET_V), F32)
            ii = lax.broadcasted_iota(jnp.int32, (t, t), 0)
            jj = lax.broadcasted_iota(jnp.int32, (t, t), 1)
            dec_ref[j] = jnp.where(jj // CHUNK <= ii // CHUNK, jnp.exp(jnp.abs(ii - jj).astype(F32) * lg), 0.0)

        q = rope(q_ref[:, cols].astype(F32))
        k = rope(k_ref[:, cols].astype(F32)) * RET_QK ** -0.5
        v = v_ref[:, cols]
        q_dec = q * jnp.exp((pos + 1.0) * lg)
        k_dec = k * jnp.exp((t - 1.0 - pos) * lg)
        a = lax.dot_general(q.astype(BF16), k.astype(BF16), (((1,), (1,)), ((), ())),
                            preferred_element_type=F32) * dec_ref[j]
        state = state_ref[j]
        o = jnp.dot(a.astype(BF16), v, preferred_element_type=F32)
        o = o + jnp.dot(q_dec.astype(BF16), state.astype(BF16), preferred_element_type=F32)
        block_decay = jnp.exp(jnp.full((1, RET_V), t * 1.0, F32) * lg)
        state_ref[j] = state * block_decay + lax.dot_general(
            k_dec.astype(BF16), v, (((0,), (0,)), ((), ())), preferred_element_type=F32)

        mu = jnp.mean(o, axis=-1, keepdims=True)
        d = o - mu
        var = jnp.mean(d * d, axis=-1, keepdims=True)
        on = d * lax.rsqrt(var + RET_GN_EPS)
        g = g_ref[:, cols].astype(F32)
        o_ref[:, cols] = (g * jax.nn.sigmoid(g) * on).astype(BF16)


def _retention(zr, c_k, s_k, b, s, t=512, hp=2):
    n = zr.shape[0]
    t = min(t, s)
    nt = s // t
    ng = RET_HEADS // hp
    log_g = jnp.log1p(-(2.0 ** (-5.0 - jnp.arange(RET_HEADS, dtype=F32))))
    col = lambda part: pl.BlockSpec((t, hp * RET_QK), lambda bi, h, ti: (bi * nt + ti, part * ng + h))
    tab = pl.BlockSpec((t, LANE), lambda bi, h, ti: (bi * nt + ti, 0))
    return pl.pallas_call(
        functools.partial(_ret_kernel, t=t, hp=hp),
        grid=(b, ng, nt),
        in_specs=[pl.BlockSpec(memory_space=pltpu.SMEM), col(0), col(1), col(2), col(3), tab, tab],
        out_specs=pl.BlockSpec((t, hp * RET_V), lambda bi, h, ti: (bi * nt + ti, h)),
        out_shape=jax.ShapeDtypeStruct((n, RET_HEADS * RET_V), BF16),
        scratch_shapes=[pltpu.VMEM((hp, RET_QK, RET_V), F32), pltpu.VMEM((hp, t, t), F32)],
        compiler_params=_params("parallel", "parallel", "arbitrary"),
        name="retention",
    )(log_g, zr, zr, zr, zr, c_k, s_k)


def _gmlp_kernel(u_ref, v_ref, lng_ref, lnb_ref, ws_ref, bst_ref, o_ref, *, nblk):
    v = jax.nn.gelu(v_ref[...].astype(F32))
    mu = jnp.mean(v, axis=-1, keepdims=True)
    d = v - mu
    var = jnp.mean(d * d, axis=-1, keepdims=True)
    vn = (d * lax.rsqrt(var + EPS) * lng_ref[...] + lnb_ref[...]).astype(BF16)
    pc_i = lax.broadcasted_iota(jnp.int32, (GM_BLOCK, GM_BLOCK), 0) // CHUNK
    pc_j = lax.broadcasted_iota(jnp.int32, (GM_BLOCK, GM_BLOCK), 1) // CHUNK
    gw = GM_WIDTH // GM_GROUPS
    for g in range(GM_GROUPS):
        w = jnp.where(pc_i >= pc_j, ws_ref[g], 0.0).astype(BF16)
        bias = bst_ref[:, g:g + 1]
        for r in range(nblk):
            rows = slice(r * GM_BLOCK, (r + 1) * GM_BLOCK)
            cols = slice(g * gw, (g + 1) * gw)
            mixed = jnp.dot(w, vn[rows, cols], preferred_element_type=F32) + bias
            u = jax.nn.gelu(u_ref[rows, cols].astype(F32))
            o_ref[rows, cols] = (u * mixed).astype(BF16)


def _gmlp(zg, ln_g, ln_b, w_s, b_s_t, l, nblk=2):
    n = zg.shape[0]
    tm = nblk * GM_BLOCK
    return pl.pallas_call(
        functools.partial(_gmlp_kernel, nblk=nblk),
        grid=(n // tm,),
        in_specs=[
            pl.BlockSpec((tm, GM_WIDTH), lambda i: (i, 0)),
            pl.BlockSpec((tm, GM_WIDTH), lambda i: (i, 1)),
            pl.BlockSpec((1, GM_WIDTH), lambda i: (0, 0)),
            pl.BlockSpec((1, GM_WIDTH), lambda i: (0, 0)),
            pl.BlockSpec((None, GM_GROUPS, GM_BLOCK, GM_BLOCK), lambda i: (l, 0, 0, 0)),
            pl.BlockSpec((GM_BLOCK, GM_GROUPS), lambda i: (0, 0)),
        ],
        out_specs=pl.BlockSpec((tm, GM_WIDTH), lambda i: (i, 0)),
        out_shape=jax.ShapeDtypeStruct((n, GM_WIDTH), BF16),
        compiler_params=_params("parallel"),
        name="gmlp",
    )(zg, zg, ln_g, ln_b, w_s, b_s_t)


def _merge_kernel(h_ref, ya_ref, yb_ref, yc_ref, wg0_ref, wg1_ref, wg2_ref, bg0_ref, bg1_ref, bg2_ref,
                  wb0_ref, wb1_ref, wb2_ref, o_ref):
    h = h_ref[...]

    def branch(y_ref, wg_ref, bg_ref, wb_ref):
        gate = jax.nn.sigmoid(jnp.dot(h, wg_ref[...], preferred_element_type=F32) + bg_ref[...])
        return gate * jnp.dot(y_ref[...], wb_ref[...], preferred_element_type=F32)

    merged = branch(ya_ref, wg0_ref, bg0_ref, wb0_ref)
    merged = merged + branch(yb_ref, wg1_ref, bg1_ref, wb1_ref)
    merged = merged + branch(yc_ref, wg2_ref, bg2_ref, wb2_ref)
    o_ref[...] = merged.astype(BF16)


def _merge(h, ya, yb, yc, w_gate, b_gate, w_br, l, tm=512, tn=512):
    n, d = h.shape
    tm = min(tm, n)
    nj = d // tn
    act = pl.BlockSpec((tm, d), lambda i, j: (i, 0))
    wg = lambda br: pl.BlockSpec((None, d, tn), lambda i, j: (l, 0, br * nj + j))
    bg = lambda br: pl.BlockSpec((1, tn), lambda i, j: (0, br * nj + j))
    wb = lambda br: pl.BlockSpec((None, None, d, tn), lambda i, j: (l, br, 0, j))
    return pl.pallas_call(
        _merge_kernel,
        grid=(n // tm, nj),
        in_specs=[act, act, act, act, wg(0), wg(1), wg(2), bg(0), bg(1), bg(2), wb(0), wb(1), wb(2)],
        out_specs=pl.BlockSpec((tm, tn), lambda i, j: (i, j)),
        out_shape=jax.ShapeDtypeStruct((n, d), BF16),
        compiler_params=_params("parallel", "arbitrary"),
        name="merge",
    )(h, ya, yb, yc, w_gate, w_gate, w_gate, b_gate, b_gate, b_gate, w_br, w_br, w_br)


def _out_proj_kernel(m_ref, w_ref, g_ref, x_ref, o_ref):
    y = jnp.dot(m_ref[...], w_ref[...], preferred_element_type=F32)
    o_ref[...] = x_ref[...] + _rms(y, g_ref[...])


def _out_proj(merged, w_o, g_post, x, l, tm=512):
    n, d = x.shape
    tm = min(tm, n)
    return pl.pallas_call(
        _out_proj_kernel,
        grid=(n // tm,),
        in_specs=[
            pl.BlockSpec((tm, d), lambda i: (i, 0)),
            pl.BlockSpec((None, d, d), lambda i: (l, 0, 0)),
            pl.BlockSpec((1, d), lambda i: (0, 0)),
            pl.BlockSpec((tm, d), lambda i: (i, 0)),
        ],
        out_specs=pl.BlockSpec((tm, d), lambda i: (i, 0)),
        out_shape=jax.ShapeDtypeStruct((n, d), F32),
        compiler_params=_params("parallel"),
        name="out_proj",
    )(merged, w_o, g_post, x)


def _rot_cols(w):
    half = w.shape[-1] // 2
    return jnp.concatenate([-w[..., half:], w[..., :half]], axis=-1)


def _cast_wi_kernel(w_ref, o_ref):
    o_ref[:, :D_FF] = w_ref[...].astype(BF16)
    o_ref[:, D_FF:] = jnp.zeros((o_ref.shape[0], D_FF_PAD - D_FF), BF16)


def _cast_wi(wi, tr=256):
    nl, d, _ = wi.shape
    return pl.pallas_call(
        _cast_wi_kernel,
        grid=(nl, 2, d // tr),
        in_specs=[pl.BlockSpec((None, tr, D_FF), lambda l, h, r: (l, r, h))],
        out_specs=pl.BlockSpec((None, None, tr, D_FF_PAD), lambda l, h, r: (l, h, r, 0)),
        out_shape=jax.ShapeDtypeStruct((nl, 2, d, D_FF_PAD), BF16),
        compiler_params=_params("parallel", "parallel", "parallel"),
        name="cast_wi",
    )(wi)


def _cast_wo_kernel(w_ref, o_ref):
    rows = pl.program_id(1) * FF_TILE + lax.broadcasted_iota(jnp.int32, (FF_TILE, 1), 0)
    o_ref[...] = jnp.where(rows < D_FF, w_ref[...], 0.0).astype(BF16)


def _cast_wo(wo):
    nl, _, d = wo.shape
    spec = pl.BlockSpec((None, FF_TILE, d), lambda l, j: (l, j, 0))
    return pl.pallas_call(
        _cast_wo_kernel,
        grid=(nl, D_FF_PAD // FF_TILE),
        in_specs=[spec],
        out_specs=spec,
        out_shape=jax.ShapeDtypeStruct((nl, D_FF_PAD, d), BF16),
        compiler_params=_params("parallel", "parallel"),
        name="cast_wo",
    )(wo)


W_IN_SHIFT = MLA_ROPE
W_IN_TILE = 1024


def _cast_w_in_kernel(a_ref, b_ref, o_ref):
    o_ref[...] = jnp.concatenate([a_ref[:, W_IN_SHIFT:], b_ref[:, :W_IN_SHIFT]], axis=1).astype(BF16)


def _cast_w_in(w_in, tr=1024):
    nl, d, _ = w_in.shape
    first = (Q_LORA + KV_LORA) // W_IN_TILE
    per_tile = W_IN_TILE // LANE
    return pl.pallas_call(
        _cast_w_in_kernel,
        grid=(nl, d // tr, (RET_GROUP_W + GM_GROUP_W) // W_IN_TILE),
        in_specs=[pl.BlockSpec((None, tr, W_IN_TILE), lambda l, r, j: (l, r, first + j)),
                  pl.BlockSpec((None, tr, LANE), lambda l, r, j: (l, r, (first + j + 1) * per_tile))],
        out_specs=pl.BlockSpec((None, tr, W_IN_TILE), lambda l, r, j: (l, r, j)),
        out_shape=jax.ShapeDtypeStruct((nl, d, RET_GROUP_W + GM_GROUP_W), BF16),
        compiler_params=_params("parallel", "parallel", "parallel"),
        name="cast_w_in",
    )(w_in, w_in)


def _cast_w_mla_kernel(a_ref, b_ref, o_ref):
    o_kr = Q_LORA + KV_LORA
    o_ref[:, :o_kr] = a_ref[...].astype(BF16)
    w_kr = b_ref[:, :MLA_ROPE]
    o_ref[:, o_kr:] = jnp.concatenate([w_kr, _rot_cols(w_kr)], axis=1).astype(BF16)


def _cast_w_mla(w_in, tr=512):
    nl, d, _ = w_in.shape
    o_kr = Q_LORA + KV_LORA
    return pl.pallas_call(
        _cast_w_mla_kernel,
        grid=(nl, d // tr),
        in_specs=[pl.BlockSpec((None, tr, o_kr), lambda l, r: (l, r, 0)),
                  pl.BlockSpec((None, tr, LANE), lambda l, r: (l, r, o_kr // LANE))],
        out_specs=pl.BlockSpec((None, tr, MLA_GROUP_W), lambda l, r: (l, r, 0)),
        out_shape=jax.ShapeDtypeStruct((nl, d, MLA_GROUP_W), BF16),
        compiler_params=_params("parallel", "parallel"),
        name="cast_w_mla",
    )(w_in, w_in)


def _prep_w_uq(w_uq):
    nl = w_uq.shape[0]
    w = w_uq.reshape(nl, Q_LORA, MLA_HEADS, MLA_NOPE + MLA_ROPE)
    w_rope = w[..., MLA_NOPE:]
    w = jnp.concatenate([w, _rot_cols(w_rope)], axis=-1)
    return w.reshape(nl, Q_LORA, MLA_HEADS * MLA_QK_PAD).astype(BF16)


def _prep_weights(p):
    w = {}
    w["ffn1"] = (_cast_wi(p["ffn1_wi"]), _cast_wo(p["ffn1_wo"]))
    w["ffn2"] = (_cast_wi(p["ffn2_wi"]), _cast_wo(p["ffn2_wo"]))
    w_in = p["w_in"].astype(BF16)
    w["w_mla"] = _cast_w_mla(w_in)
    w["w_rg"] = _cast_w_in(w_in)
    w["w_uq"] = _prep_w_uq(p["w_uq"])
    for name in ("w_ukv", "w_gate", "w_br", "w_o"):
        w[name] = p[name].astype(BF16)
    w["gm_b_s_t"] = jnp.swapaxes(p["gm_b_s"], 1, 2)
    return w


def _row(v):
    return v.reshape(1, -1)


def _token_mixer(x, l, p, w, tables, b, s):
    c_r, s_r, c_k, s_k = tables
    zm, h = _norm_mm(x, _row(p["mix_pre_g"][l]), w["w_mla"], l)
    zr = _mm(h, w["w_rg"], l, 0, RET_GROUP_W)
    zg = _mm(h, w["w_rg"], l, RET_GROUP_W, GM_GROUP_W)
    q, kv, kr = _mla_prep(zm, _row(p["q_norm_g"][l]), _row(p["kv_norm_g"][l]), w["w_uq"], w["w_ukv"], c_r, s_r, l)
    y_a = _attention(q, kv, kr, b, s)
    y_b = _retention(zr, c_k, s_k, b, s)
    y_c = _gmlp(zg, _row(p["gm_ln_g"][l]), _row(p["gm_ln_b"][l]), p["gm_w_s"], w["gm_b_s_t"][l], l)
    merged = _merge(h, y_a, y_b, y_c, w["w_gate"], _row(p["b_gate"][l]), w["w_br"], l)
    return _out_proj(merged, w["w_o"], _row(p["mix_post_g"][l]), x, l)


def _trunk(x, pos, p, depth):
    b, s, d = x.shape
    tables = _rope_tables(pos)
    w = _prep_weights(p)
    x = x.reshape(b * s, d)
    for l in range(depth):
        x = _ffn(x, _row(p["ffn1_pre_g"][l]), *w["ffn1"], _row(p["ffn1_post_g"][l]), l)
        x = _token_mixer(x, l, p, w, tables, b, s)
        x = _ffn(x, _row(p["ffn2_pre_g"][l]), *w["ffn2"], _row(p["ffn2_post_g"][l]), l)
    return x.reshape(b, s, d)


def kernel(x, pos, ffn1_pre_g, ffn1_wi, ffn1_wo, ffn1_post_g, mix_pre_g, w_in, q_norm_g, w_uq, kv_norm_g, w_ukv, gm_ln_g, gm_ln_b, gm_w_s, gm_b_s, w_gate, b_gate, w_br, w_o, mix_post_g, ffn2_pre_g, ffn2_wi, ffn2_wo, ffn2_post_g):
    p = dict(ffn1_pre_g=ffn1_pre_g, ffn1_wi=ffn1_wi, ffn1_wo=ffn1_wo, ffn1_post_g=ffn1_post_g, mix_pre_g=mix_pre_g,
             w_in=w_in, q_norm_g=q_norm_g, w_uq=w_uq, kv_norm_g=kv_norm_g, w_ukv=w_ukv, gm_ln_g=gm_ln_g,
             gm_ln_b=gm_ln_b, gm_w_s=gm_w_s, gm_b_s=gm_b_s, w_gate=w_gate, b_gate=b_gate, w_br=w_br, w_o=w_o,
             mix_post_g=mix_post_g, ffn2_pre_g=ffn2_pre_g, ffn2_wi=ffn2_wi, ffn2_wo=ffn2_wo, ffn2_post_g=ffn2_post_g)
    return _trunk(x, pos, p, DEPTH)
```

```python
import functools

import numpy as np
import jax
import jax.numpy as jnp
from jax import lax
from jax.experimental import pallas as pl
from jax.experimental.pallas import tpu as pltpu

F32 = jnp.float32
BF16 = jnp.bfloat16

D_MODEL = 2048
DEPTH = 4
CHUNK = 64
EPS = 1e-6
ROPE_BASE = 10000.0
MLA_HEADS = 16
MLA_NOPE = 128
MLA_ROPE = 64
MLA_V = 128
Q_LORA = 512
KV_LORA = 512
RET_HEADS = 8
RET_QK = 256
RET_V = 256
RET_GN_EPS = 1e-5
GM_GROUPS = 4
GM_WIDTH = 2048
GM_BLOCK = 128
N_BRANCH = 3
D_FF = 5504

LANE = 128
MLA_QK_PAD = 2 * LANE
MLA_GROUP_W = Q_LORA + KV_LORA + LANE
RET_GROUP_W = 4 * RET_HEADS * RET_QK
GM_GROUP_W = 2 * GM_WIDTH
FF_TILE = 512
D_FF_PAD = ((D_FF + FF_TILE - 1) // FF_TILE) * FF_TILE
VMEM_LIMIT = 56 * 2 ** 20


def _params(*sem, flags=None):
    return pltpu.CompilerParams(dimension_semantics=sem, vmem_limit_bytes=VMEM_LIMIT, flags=flags)


def _rms(x, g):
    return x * lax.rsqrt(jnp.mean(x * x, axis=-1, keepdims=True) + EPS) * g


def _rope_table_kernel(ang_r_ref, ang_k_ref, cr_ref, sr_ref, ck_ref, sk_ref):
    a = ang_r_ref[...]
    live = lax.broadcasted_iota(jnp.int32, a.shape, 1) < MLA_ROPE
    cr_ref[...] = jnp.where(live, jnp.cos(a), 0.0)
    sr_ref[...] = jnp.where(live, jnp.sin(a), 0.0)
    k = ang_k_ref[...]
    ck_ref[...] = jnp.cos(k)
    sk_ref[...] = jnp.sin(k)


def _rope_tables(pos):
    n = pos.size
    p = pos.astype(F32).reshape(n, 1)
    inv_r = ROPE_BASE ** (-jnp.arange(0, MLA_ROPE, 2, dtype=F32) / MLA_ROPE)
    inv_k = ROPE_BASE ** (-jnp.arange(0, RET_QK, 2, dtype=F32) / RET_QK)
    ang_r = p * inv_r
    ang_r = jnp.concatenate([ang_r, ang_r, jnp.zeros((n, LANE - MLA_ROPE), F32)], axis=1)
    ang_k = p * inv_k
    tm = min(n, 1024)
    spec = pl.BlockSpec((tm, LANE), lambda i: (i, 0))
    out = jax.ShapeDtypeStruct((n, LANE), F32)
    return pl.pallas_call(
        _rope_table_kernel,
        grid=(n // tm,),
        in_specs=[spec, spec],
        out_specs=[spec] * 4,
        out_shape=[out] * 4,
        compiler_params=_params("parallel"),
        name="rope_tables",
    )(ang_r, ang_k)


def _ffn_kernel(x_ref, gpre_ref, wa_ref, wb_ref, wo_ref, gpost_ref, o_ref, h_ref):
    f = pl.program_id(1)

    def hidden_tile(h):
        a = jnp.dot(h, wa_ref[...], preferred_element_type=F32)
        b = jnp.dot(h, wb_ref[...], preferred_element_type=F32)
        act = (a * jax.nn.sigmoid(a) * b).astype(BF16)
        return jnp.dot(act, wo_ref[...], preferred_element_type=F32)

    @pl.when(f == 0)
    def _():
        h = _rms(x_ref[...], gpre_ref[...]).astype(BF16)
        h_ref[...] = h
        o_ref[...] = hidden_tile(h)

    @pl.when(f > 0)
    def _():
        o_ref[...] += hidden_tile(h_ref[...])

    @pl.when(f == pl.num_programs(1) - 1)
    def _():
        o_ref[...] = x_ref[...] + _rms(o_ref[...], 0.5 * gpost_ref[...])


def _ffn(x, g_pre, wi, wo, g_post, l, tm=1024):
    n, d = x.shape
    nf = D_FF_PAD // FF_TILE
    tm = min(tm, n)
    return pl.pallas_call(
        _ffn_kernel,
        grid=(n // tm, nf),
        in_specs=[
            pl.BlockSpec((tm, d), lambda i, f: (i, 0)),
            pl.BlockSpec((1, d), lambda i, f: (0, 0)),
            pl.BlockSpec((None, None, d, FF_TILE), lambda i, f: (l, 0, 0, f)),
            pl.BlockSpec((None, None, d, FF_TILE), lambda i, f: (l, 1, 0, f)),
            pl.BlockSpec((None, FF_TILE, d), lambda i, f: (l, f, 0)),
            pl.BlockSpec((1, d), lambda i, f: (0, 0)),
        ],
        out_specs=pl.BlockSpec((tm, d), lambda i, f: (i, 0)),
        out_shape=jax.ShapeDtypeStruct((n, d), F32),
        scratch_shapes=[pltpu.VMEM((tm, d), BF16)],
        compiler_params=_params("parallel", "arbitrary"),
        name="ffn",
    )(x, g_pre, wi, wi, wo, g_post)


def _norm_mm_kernel(x_ref, g_ref, w_ref, o_ref, h_ref):
    h = _rms(x_ref[...], g_ref[...]).astype(BF16)
    h_ref[...] = h
    o_ref[...] = jnp.dot(h, w_ref[...], preferred_element_type=F32).astype(o_ref.dtype)


def _norm_mm(x, g, w, l, tm=1024):
    n, k = x.shape
    nc = w.shape[2]
    tm = min(tm, n)
    return pl.pallas_call(
        _norm_mm_kernel,
        grid=(n // tm,),
        in_specs=[pl.BlockSpec((tm, k), lambda i: (i, 0)), pl.BlockSpec((1, k), lambda i: (0, 0)),
                  pl.BlockSpec((None, k, nc), lambda i: (l, 0, 0))],
        out_specs=[pl.BlockSpec((tm, nc), lambda i: (i, 0)), pl.BlockSpec((tm, k), lambda i: (i, 0))],
        out_shape=[jax.ShapeDtypeStruct((n, nc), BF16), jax.ShapeDtypeStruct((n, k), BF16)],
        compiler_params=_params("parallel"),
        name="norm_in_proj",
    )(x, g, w)


def _mm_kernel(a_ref, w_ref, o_ref):
    o_ref[...] = jnp.dot(a_ref[...], w_ref[...], preferred_element_type=F32).astype(o_ref.dtype)


def _mm(a, w, l, col0, nc, tn=1024, tm=2048):
    n, k = a.shape
    j0 = col0 // tn
    tm = min(tm, n)
    return pl.pallas_call(
        _mm_kernel,
        grid=(n // tm, nc // tn),
        in_specs=[pl.BlockSpec((tm, k), lambda i, j: (i, 0)), pl.BlockSpec((None, k, tn), lambda i, j: (l, 0, j0 + j))],
        out_specs=pl.BlockSpec((tm, tn), lambda i, j: (i, j)),
        out_shape=jax.ShapeDtypeStruct((n, nc), BF16),
        compiler_params=_params("parallel", "arbitrary"),
        name="in_proj",
    )(a, w)


def _rope_half_block(blk, c, s):
    return blk * c + pltpu.roll(blk, MLA_ROPE, axis=1) * s


def _mla_prep_kernel(z_ref, gq_ref, gkv_ref, wq_ref, wkv_ref, c_ref, s_ref, q_ref, kv_ref, kr_ref):
    c = c_ref[...]
    s = s_ref[...]
    hq = _rms(z_ref[:, :Q_LORA].astype(F32), gq_ref[...]).astype(BF16)
    for h in range(MLA_HEADS):
        lo = h * MLA_QK_PAD
        qh = jnp.dot(hq, wq_ref[:, lo:lo + MLA_QK_PAD], preferred_element_type=F32)
        q_ref[:, lo:lo + LANE] = qh[:, :LANE].astype(BF16)
        q_ref[:, lo + LANE:lo + MLA_QK_PAD] = _rope_half_block(qh[:, LANE:], c, s).astype(BF16)
    hkv = _rms(z_ref[:, Q_LORA:Q_LORA + KV_LORA].astype(F32), gkv_ref[...]).astype(BF16)
    step = 4 * (MLA_NOPE + MLA_V)
    for lo in range(0, MLA_HEADS * (MLA_NOPE + MLA_V), step):
        kv_ref[:, lo:lo + step] = jnp.dot(hkv, wkv_ref[:, lo:lo + step], preferred_element_type=F32).astype(BF16)
    kr_ref[...] = _rope_half_block(z_ref[:, Q_LORA + KV_LORA:].astype(F32), c, s).astype(BF16)


def _mla_prep(zm, gq, gkv, wq, wkv, c_r, s_r, l, tm=512):
    n = zm.shape[0]
    tm = min(tm, n)
    wq_w = MLA_HEADS * MLA_QK_PAD
    wkv_w = MLA_HEADS * (MLA_NOPE + MLA_V)
    row = lambda w: pl.BlockSpec((tm, w), lambda i: (i, 0))
    full = lambda r, w: pl.BlockSpec((r, w), lambda i: (0, 0))
    layer = lambda r, w: pl.BlockSpec((None, r, w), lambda i: (l, 0, 0))
    return pl.pallas_call(
        _mla_prep_kernel,
        grid=(n // tm,),
        in_specs=[row(MLA_GROUP_W), full(1, Q_LORA), full(1, KV_LORA), layer(Q_LORA, wq_w), layer(KV_LORA, wkv_w),
                  row(LANE), row(LANE)],
        out_specs=[row(wq_w), row(wkv_w), row(LANE)],
        out_shape=[jax.ShapeDtypeStruct((n, wq_w), BF16), jax.ShapeDtypeStruct((n, wkv_w), BF16),
                   jax.ShapeDtypeStruct((n, LANE), BF16)],
        compiler_params=_params("parallel"),
        name="mla_prep",
    )(zm, gq, gkv, wq, wkv, c_r, s_r)


def _attn_kernel(q_ref, kv_ref, kr_ref, o_ref, *, tq, tk, hp):
    i = pl.program_id(2)
    c = (MLA_NOPE + MLA_ROPE) ** -0.5 * np.log2(np.e)
    hw = MLA_NOPE + MLA_V

    def tile(k0, carry, width, masked_from):
        kr = kr_ref[pl.ds(k0, width), :]
        if masked_from is not None:
            mw = width - masked_from
            qc = lax.broadcasted_iota(jnp.int32, (tq, mw), 0) // CHUNK
            kc = lax.broadcasted_iota(jnp.int32, (tq, mw), 1) // CHUNK
            mask = kc <= qc
        new = []
        for j in range(hp):
            m, l, acc = carry[j]
            q = q_ref[:, j * MLA_QK_PAD:(j + 1) * MLA_QK_PAD]
            k = jnp.concatenate([kv_ref[pl.ds(k0, width), j * hw:j * hw + MLA_NOPE], kr], axis=1)
            s = lax.dot_general(q, k, (((1,), (1,)), ((), ())), preferred_element_type=F32)
            if masked_from is not None:
                tail = jnp.where(mask, s[:, masked_from:], -1e30)
                s = tail if masked_from == 0 else jnp.concatenate([s[:, :masked_from], tail], axis=1)
            m_new = jnp.maximum(m, jnp.max(s, axis=-1, keepdims=True))
            alpha = jnp.exp2((m - m_new) * c)
            p = jnp.exp2((s - m_new) * c)
            l = alpha * l + jnp.sum(p, axis=-1, keepdims=True)
            v = kv_ref[pl.ds(k0, width), j * hw + MLA_NOPE:(j + 1) * hw]
            acc = alpha * acc + jnp.dot(p.astype(BF16), v, preferred_element_type=F32)
            new.append((m_new, l, acc))
        return tuple(new)

    def body(kb, carry):
        return tile(pl.multiple_of(kb * tk, tk), carry, tk, None)

    def finish(carry):
        for j in range(hp):
            m, l, acc = carry[j]
            o_ref[:, j * MLA_V:(j + 1) * MLA_V] = (acc / l).astype(BF16)

    carry = tuple((jnp.full((tq, 1), -1e30, F32), jnp.zeros((tq, 1), F32), jnp.zeros((tq, MLA_V), F32))
                  for _ in range(hp))
    q0 = i * tq
    n_full = q0 // tk
    carry = lax.fori_loop(0, n_full, body, carry)
    for lead in range(0, tk, tq):
        @pl.when(q0 - n_full * tk == lead)
        def _():
            finish(tile(pl.multiple_of(q0 - lead, tq), carry, lead + tq, lead))


def _attention(q, kv, kr, b, s, tq=512, tk=4096, hp=4):
    n = q.shape[0]
    tq = min(tq, s)
    tk = min(tk, s)
    assert tk % tq == 0
    nq = s // tq
    kv3 = kv.reshape(b, s, kv.shape[1])
    kr3 = kr.reshape(b, s, LANE)
    return pl.pallas_call(
        functools.partial(_attn_kernel, tq=tq, tk=tk, hp=hp),
        grid=(b, MLA_HEADS // hp, nq),
        in_specs=[
            pl.BlockSpec((tq, hp * MLA_QK_PAD), lambda bi, h, i: (bi * nq + i, h)),
            pl.BlockSpec((None, s, hp * (MLA_NOPE + MLA_V)), lambda bi, h, i: (bi, 0, h)),
            pl.BlockSpec((None, s, LANE), lambda bi, h, i: (bi, 0, 0)),
        ],
        out_specs=pl.BlockSpec((tq, hp * MLA_V), lambda bi, h, i: (bi * nq + i, h)),
        out_shape=jax.ShapeDtypeStruct((n, MLA_HEADS * MLA_V), BF16),
        compiler_params=_params("parallel", "parallel", "arbitrary"),
        name="mla_attention",
    )(q, kv3, kr3)


def _ret_kernel(lg_ref, q_ref, k_ref, v_ref, g_ref, c_ref, s_ref, o_ref, state_ref, dec_ref, *, t, hp):
    first = pl.program_id(2) == 0
    c = c_ref[...]
    s = s_ref[...]
    half = RET_QK // 2
    pos = lax.broadcasted_iota(jnp.int32, (t, 1), 0).astype(F32)

    def rope(x):
        x1 = x[:, :half]
        x2 = x[:, half:]
        return jnp.concatenate([x1 * c - x2 * s, x2 * c + x1 * s], axis=1)

    for j in range(hp):
        lg = lg_ref[pl.program_id(1) * hp + j]
        cols = slice(j * RET_QK, (j + 1) * RET_QK)

        @pl.when(first)
        def _():
            state_ref[j] = jnp.zeros((RET_QK, RET_V), F32)
            ii = lax.broadcasted_iota(jnp.int32, (t, t), 0)
            jj = lax.broadcasted_iota(jnp.int32, (t, t), 1)
            dec_ref[j] = jnp.where(jj // CHUNK <= ii // CHUNK, jnp.exp(jnp.abs(ii - jj).astype(F32) * lg), 0.0)

        q = rope(q_ref[:, cols].astype(F32))
        k = rope(k_ref[:, cols].astype(F32)) * RET_QK ** -0.5
        v = v_ref[:, cols]
        q_dec = q * jnp.exp((pos + 1.0) * lg)
        k_dec = k * jnp.exp((t - 1.0 - pos) * lg)
        a = lax.dot_general(q.astype(BF16), k.astype(BF16), (((1,), (1,)), ((), ())),
                            preferred_element_type=F32) * dec_ref[j]
        state = state_ref[j]
        o = jnp.dot(a.astype(BF16), v, preferred_element_type=F32)
        o = o + jnp.dot(q_dec.astype(BF16), state.astype(BF16), preferred_element_type=F32)
        block_decay = jnp.exp(jnp.full((1, RET_V), t * 1.0, F32) * lg)
        state_ref[j] = state * block_decay + lax.dot_general(
            k_dec.astype(BF16), v, (((0,), (0,)), ((), ())), preferred_element_type=F32)

        mu = jnp.mean(o, axis=-1, keepdims=True)
        d = o - mu
        var = jnp.mean(d * d, axis=-1, keepdims=True)
        on = d * lax.rsqrt(var + RET_GN_EPS)
        g = g_ref[:, cols].astype(F32)
        o_ref[:, cols] = (g * jax.nn.sigmoid(g) * on).astype(BF16)


def _retention(zr, c_k, s_k, b, s, t=512, hp=2):
    n = zr.shape[0]
    t = min(t, s)
    nt = s // t
    ng = RET_HEADS // hp
    log_g = jnp.log1p(-(2.0 ** (-5.0 - jnp.arange(RET_HEADS, dtype=F32))))
    col = lambda part: pl.BlockSpec((t, hp * RET_QK), lambda bi, h, ti: (bi * nt + ti, part * ng + h))
    tab = pl.BlockSpec((t, LANE), lambda bi, h, ti: (bi * nt + ti, 0))
    return pl.pallas_call(
        functools.partial(_ret_kernel, t=t, hp=hp),
        grid=(b, ng, nt),
        in_specs=[pl.BlockSpec(memory_space=pltpu.SMEM), col(0), col(1), col(2), col(3), tab, tab],
        out_specs=pl.BlockSpec((t, hp * RET_V), lambda bi, h, ti: (bi * nt + ti, h)),
        out_shape=jax.ShapeDtypeStruct((n, RET_HEADS * RET_V), BF16),
        scratch_shapes=[pltpu.VMEM((hp, RET_QK, RET_V), F32), pltpu.VMEM((hp, t, t), F32)],
        compiler_params=_params("parallel", "parallel", "arbitrary"),
        name="retention",
    )(log_g, zr, zr, zr, zr, c_k, s_k)


def _gmlp_kernel(u_ref, v_ref, lng_ref, lnb_ref, ws_ref, bst_ref, o_ref, *, nblk):
    v = jax.nn.gelu(v_ref[...].astype(F32))
    mu = jnp.mean(v, axis=-1, keepdims=True)
    d = v - mu
    var = jnp.mean(d * d, axis=-1, keepdims=True)
    vn = (d * lax.rsqrt(var + EPS) * lng_ref[...] + lnb_ref[...]).astype(BF16)
    pc_i = lax.broadcasted_iota(jnp.int32, (GM_BLOCK, GM_BLOCK), 0) // CHUNK
    pc_j = lax.broadcasted_iota(jnp.int32, (GM_BLOCK, GM_BLOCK), 1) // CHUNK
    gw = GM_WIDTH // GM_GROUPS
    for g in range(GM_GROUPS):
        w = jnp.where(pc_i >= pc_j, ws_ref[g], 0.0).astype(BF16)
        bias = bst_ref[:, g:g + 1]
        for r in range(nblk):
            rows = slice(r * GM_BLOCK, (r + 1) * GM_BLOCK)
            cols = slice(g * gw, (g + 1) * gw)
            mixed = jnp.dot(w, vn[rows, cols], preferred_element_type=F32) + bias
            u = jax.nn.gelu(u_ref[rows, cols].astype(F32))
            o_ref[rows, cols] = (u * mixed).astype(BF16)


def _gmlp(zg, ln_g, ln_b, w_s, b_s_t, l, nblk=2):
    n = zg.shape[0]
    tm = nblk * GM_BLOCK
    return pl.pallas_call(
        functools.partial(_gmlp_kernel, nblk=nblk),
        grid=(n // tm,),
        in_specs=[
            pl.BlockSpec((tm, GM_WIDTH), lambda i: (i, 0)),
            pl.BlockSpec((tm, GM_WIDTH), lambda i: (i, 1)),
            pl.BlockSpec((1, GM_WIDTH), lambda i: (0, 0)),
            pl.BlockSpec((1, GM_WIDTH), lambda i: (0, 0)),
            pl.BlockSpec((None, GM_GROUPS, GM_BLOCK, GM_BLOCK), lambda i: (l, 0, 0, 0)),
            pl.BlockSpec((GM_BLOCK, GM_GROUPS), lambda i: (0, 0)),
        ],
        out_specs=pl.BlockSpec((tm, GM_WIDTH), lambda i: (i, 0)),
        out_shape=jax.ShapeDtypeStruct((n, GM_WIDTH), BF16),
        compiler_params=_params("parallel"),
        name="gmlp",
    )(zg, zg, ln_g, ln_b, w_s, b_s_t)


def _merge_kernel(h_ref, ya_ref, yb_ref, yc_ref, wg0_ref, wg1_ref, wg2_ref, bg0_ref, bg1_ref, bg2_ref,
                  wb0_ref, wb1_ref, wb2_ref, o_ref):
    h = h_ref[...]

    def branch(y_ref, wg_ref, bg_ref, wb_ref):
        gate = jax.nn.sigmoid(jnp.dot(h, wg_ref[...], preferred_element_type=F32) + bg_ref[...])
        return gate * jnp.dot(y_ref[...], wb_ref[...], preferred_element_type=F32)

    merged = branch(ya_ref, wg0_ref, bg0_ref, wb0_ref)
    merged = merged + branch(yb_ref, wg1_ref, bg1_ref, wb1_ref)
    merged = merged + branch(yc_ref, wg2_ref, bg2_ref, wb2_ref)
    o_ref[...] = merged.astype(BF16)


def _merge(h, ya, yb, yc, w_gate, b_gate, w_br, l, tm=512, tn=512):
    n, d = h.shape
    tm = min(tm, n)
    nj = d // tn
    act = pl.BlockSpec((tm, d), lambda i, j: (i, 0))
    wg = lambda br: pl.BlockSpec((None, d, tn), lambda i, j: (l, 0, br * nj + j))
    bg = lambda br: pl.BlockSpec((1, tn), lambda i, j: (0, br * nj + j))
    wb = lambda br: pl.BlockSpec((None, None, d, tn), lambda i, j: (l, br, 0, j))
    return pl.pallas_call(
        _merge_kernel,
        grid=(n // tm, nj),
        in_specs=[act, act, act, act, wg(0), wg(1), wg(2), bg(0), bg(1), bg(2), wb(0), wb(1), wb(2)],
        out_specs=pl.BlockSpec((tm, tn), lambda i, j: (i, j)),
        out_shape=jax.ShapeDtypeStruct((n, d), BF16),
        compiler_params=_params("parallel", "arbitrary"),
        name="merge",
    )(h, ya, yb, yc, w_gate, w_gate, w_gate, b_gate, b_gate, b_gate, w_br, w_br, w_br)


def _out_proj_kernel(m_ref, w_ref, g_ref, x_ref, o_ref):
    y = jnp.dot(m_ref[...], w_ref[...], preferred_element_type=F32)
    o_ref[...] = x_ref[...] + _rms(y, g_ref[...])


def _out_proj(merged, w_o, g_post, x, l, tm=512):
    n, d = x.shape
    tm = min(tm, n)
    return pl.pallas_call(
        _out_proj_kernel,
        grid=(n // tm,),
        in_specs=[
            pl.BlockSpec((tm, d), lambda i: (i, 0)),
            pl.BlockSpec((None, d, d), lambda i: (l, 0, 0)),
            pl.BlockSpec((1, d), lambda i: (0, 0)),
            pl.BlockSpec((tm, d), lambda i: (i, 0)),
        ],
        out_specs=pl.BlockSpec((tm, d), lambda i: (i, 0)),
        out_shape=jax.ShapeDtypeStruct((n, d), F32),
        compiler_params=_params("parallel"),
        name="out_proj",
    )(merged, w_o, g_post, x)


def _rot_cols(w):
    half = w.shape[-1] // 2
    return jnp.concatenate([-w[..., half:], w[..., :half]], axis=-1)


def _cast_wi_kernel(w_ref, o_ref):
    o_ref[:, :D_FF] = w_ref[...].astype(BF16)
    o_ref[:, D_FF:] = jnp.zeros((o_ref.shape[0], D_FF_PAD - D_FF), BF16)


def _cast_wi(wi, tr=256):
    nl, d, _ = wi.shape
    return pl.pallas_call(
        _cast_wi_kernel,
        grid=(nl, 2, d // tr),
        in_specs=[pl.BlockSpec((None, tr, D_FF), lambda l, h, r: (l, r, h))],
        out_specs=pl.BlockSpec((None, None, tr, D_FF_PAD), lambda l, h, r: (l, h, r, 0)),
        out_shape=jax.ShapeDtypeStruct((nl, 2, d, D_FF_PAD), BF16),
        compiler_params=_params("parallel", "parallel", "parallel"),
        name="cast_wi",
    )(wi)


def _cast_wo_kernel(w_ref, o_ref):
    rows = pl.program_id(1) * FF_TILE + lax.broadcasted_iota(jnp.int32, (FF_TILE, 1), 0)
    o_ref[...] = jnp.where(rows < D_FF, w_ref[...], 0.0).astype(BF16)


def _cast_wo(wo):
    nl, _, d = wo.shape
    spec = pl.BlockSpec((None, FF_TILE, d), lambda l, j: (l, j, 0))
    return pl.pallas_call(
        _cast_wo_kernel,
        grid=(nl, D_FF_PAD // FF_TILE),
        in_specs=[spec],
        out_specs=spec,
        out_shape=jax.ShapeDtypeStruct((nl, D_FF_PAD, d), BF16),
        compiler_params=_params("parallel", "parallel"),
        name="cast_wo",
    )(wo)


W_IN_SHIFT = MLA_ROPE
W_IN_TILE = 1024


def _cast_w_in_kernel(a_ref, b_ref, o_ref):
    o_ref[...] = jnp.concatenate([a_ref[:, W_IN_SHIFT:], b_ref[:, :W_IN_SHIFT]], axis=1).astype(BF16)


def _cast_w_in(w_in, tr=1024):
    nl, d, _ = w_in.shape
    first = (Q_LORA + KV_LORA) // W_IN_TILE
    per_tile = W_IN_TILE // LANE
    return pl.pallas_call(
        _cast_w_in_kernel,
        grid=(nl, d // tr, (RET_GROUP_W + GM_GROUP_W) // W_IN_TILE),
        in_specs=[pl.BlockSpec((None, tr, W_IN_TILE), lambda l, r, j: (l, r, first + j)),
                  pl.BlockSpec((None, tr, LANE), lambda l, r, j: (l, r, (first + j + 1) * per_tile))],
        out_specs=pl.BlockSpec((None, tr, W_IN_TILE), lambda l, r, j: (l, r, j)),
        out_shape=jax.ShapeDtypeStruct((nl, d, RET_GROUP_W + GM_GROUP_W), BF16),
        compiler_params=_params("parallel", "parallel", "parallel"),
        name="cast_w_in",
    )(w_in, w_in)


def _cast_w_mla_kernel(a_ref, b_ref, o_ref):
    o_kr = Q_LORA + KV_LORA
    o_ref[:, :o_kr] = a_ref[...].astype(BF16)
    w_kr = b_ref[:, :MLA_ROPE]
    o_ref[:, o_kr:] = jnp.concatenate([w_kr, _rot_cols(w_kr)], axis=1).astype(BF16)


def _cast_w_mla(w_in, tr=512):
    nl, d, _ = w_in.shape
    o_kr = Q_LORA + KV_LORA
    return pl.pallas_call(
        _cast_w_mla_kernel,
        grid=(nl, d // tr),
        in_specs=[pl.BlockSpec((None, tr, o_kr), lambda l, r: (l, r, 0)),
                  pl.BlockSpec((None, tr, LANE), lambda l, r: (l, r, o_kr // LANE))],
        out_specs=pl.BlockSpec((None, tr, MLA_GROUP_W), lambda l, r: (l, r, 0)),
        out_shape=jax.ShapeDtypeStruct((nl, d, MLA_GROUP_W), BF16),
        compiler_params=_params("parallel", "parallel"),
        name="cast_w_mla",
    )(w_in, w_in)


def _prep_w_uq(w_uq):
    nl = w_uq.shape[0]
    w = w_uq.reshape(nl, Q_LORA, MLA_HEADS, MLA_NOPE + MLA_ROPE)
    w_rope = w[..., MLA_NOPE:]
    w = jnp.concatenate([w, _rot_cols(w_rope)], axis=-1)
    return w.reshape(nl, Q_LORA, MLA_HEADS * MLA_QK_PAD).astype(BF16)


def _prep_weights(p):
    w = {}
    w["ffn1"] = (_cast_wi(p["ffn1_wi"]), _cast_wo(p["ffn1_wo"]))
    w["ffn2"] = (_cast_wi(p["ffn2_wi"]), _cast_wo(p["ffn2_wo"]))
    w_in = p["w_in"].astype(BF16)
    w["w_mla"] = _cast_w_mla(w_in)
    w["w_rg"] = _cast_w_in(w_in)
    w["w_uq"] = _prep_w_uq(p["w_uq"])
    for name in ("w_ukv", "w_gate", "w_br", "w_o"):
        w[name] = p[name].astype(BF16)
    w["gm_b_s_t"] = jnp.swapaxes(p["gm_b_s"], 1, 2)
    return w


def _row(v):
    return v.reshape(1, -1)


def _token_mixer(x, l, p, w, tables, b, s):
    c_r, s_r, c_k, s_k = tables
    zm, h = _norm_mm(x, _row(p["mix_pre_g"][l]), w["w_mla"], l)
    zr = _mm(h, w["w_rg"], l, 0, RET_GROUP_W)
    zg = _mm(h, w["w_rg"], l, RET_GROUP_W, GM_GROUP_W)
    q, kv, kr = _mla_prep(zm, _row(p["q_norm_g"][l]), _row(p["kv_norm_g"][l]), w["w_uq"], w["w_ukv"], c_r, s_r, l)
    y_a = _attention(q, kv, kr, b, s)
    y_b = _retention(zr, c_k, s_k, b, s)
    y_c = _gmlp(zg, _row(p["gm_ln_g"][l]), _row(p["gm_ln_b"][l]), p["gm_w_s"], w["gm_b_s_t"][l], l)
    merged = _merge(h, y_a, y_b, y_c, w["w_gate"], _row(p["b_gate"][l]), w["w_br"], l)
    return _out_proj(merged, w["w_o"], _row(p["mix_post_g"][l]), x, l)


def _trunk(x, pos, p, depth):
    b, s, d = x.shape
    tables = _rope_tables(pos)
    w = _prep_weights(p)
    x = x.reshape(b * s, d)
    for l in range(depth):
        x = _ffn(x, _row(p["ffn1_pre_g"][l]), *w["ffn1"], _row(p["ffn1_post_g"][l]), l)
        x = _token_mixer(x, l, p, w, tables, b, s)
        x = _ffn(x, _row(p["ffn2_pre_g"][l]), *w["ffn2"], _row(p["ffn2_post_g"][l]), l)
    return x.reshape(b, s, d)


def kernel(x, pos, ffn1_pre_g, ffn1_wi, ffn1_wo, ffn1_post_g, mix_pre_g, w_in, q_norm_g, w_uq, kv_norm_g, w_ukv, gm_ln_g, gm_ln_b, gm_w_s, gm_b_s, w_gate, b_gate, w_br, w_o, mix_post_g, ffn2_pre_g, ffn2_wi, ffn2_wo, ffn2_post_g):
    p = dict(ffn1_pre_g=ffn1_pre_g, ffn1_wi=ffn1_wi, ffn1_wo=ffn1_wo, ffn1_post_g=ffn1_post_g, mix_pre_g=mix_pre_g,
             w_in=w_in, q_norm_g=q_norm_g, w_uq=w_uq, kv_norm_g=kv_norm_g, w_ukv=w_ukv, gm_ln_g=gm_ln_g,
             gm_ln_b=gm_ln_b, gm_w_s=gm_w_s, gm_b_s=gm_b_s, w_gate=w_gate, b_gate=b_gate, w_br=w_br, w_o=w_o,
             mix_post_g=mix_post_g, ffn2_pre_g=ffn2_pre_g, ffn2_wi=ffn2_wi, ffn2_wo=ffn2_wo, ffn2_post_g=ffn2_post_g)
    return _trunk(x, pos, p, DEPTH)
```

```python
import functools

import numpy as np
import jax
import jax.numpy as jnp
from jax import lax
from jax.experimental import pallas as pl
from jax.experimental.pallas import tpu as pltpu

F32 = jnp.float32
BF16 = jnp.bfloat16

D_MODEL = 2048
DEPTH = 4
CHUNK = 64
EPS = 1e-6
ROPE_BASE = 10000.0
MLA_HEADS = 16
MLA_NOPE = 128
MLA_ROPE = 64
MLA_V = 128
Q_LORA = 512
KV_LORA = 512
RET_HEADS = 8
RET_QK = 256
RET_V = 256
RET_GN_EPS = 1e-5
GM_GROUPS = 4
GM_WIDTH = 2048
GM_BLOCK = 128
N_BRANCH = 3
D_FF = 5504

LANE = 128
MLA_QK_PAD = 2 * LANE
MLA_GROUP_W = Q_LORA + KV_LORA + LANE
RET_GROUP_W = 4 * RET_HEADS * RET_QK
GM_GROUP_W = 2 * GM_WIDTH
FF_TILE = 512
CAST_ROWS = 32
D_FF_PAD = ((D_FF + FF_TILE - 1) // FF_TILE) * FF_TILE
VMEM_LIMIT = 56 * 2 ** 20
VMEM_LIMIT_FFN_CAST = 60 * 2 ** 20


def _params(*sem, vmem=VMEM_LIMIT):
    return pltpu.CompilerParams(dimension_semantics=sem, vmem_limit_bytes=vmem)


def _rms(x, g):
    return x * lax.rsqrt(jnp.mean(x * x, axis=-1, keepdims=True) + EPS) * g


def _rope_table_kernel(ang_r_ref, ang_k_ref, cr_ref, sr_ref, ck_ref, sk_ref):
    a = ang_r_ref[...]
    live = lax.broadcasted_iota(jnp.int32, a.shape, 1) < MLA_ROPE
    cr_ref[...] = jnp.where(live, jnp.cos(a), 0.0)
    sr_ref[...] = jnp.where(live, jnp.sin(a), 0.0)
    k = ang_k_ref[...]
    ck_ref[...] = jnp.cos(k)
    sk_ref[...] = jnp.sin(k)


def _rope_tables(pos):
    n = pos.size
    p = pos.astype(F32).reshape(n, 1)
    inv_r = ROPE_BASE ** (-jnp.arange(0, MLA_ROPE, 2, dtype=F32) / MLA_ROPE)
    inv_k = ROPE_BASE ** (-jnp.arange(0, RET_QK, 2, dtype=F32) / RET_QK)
    ang_r = p * inv_r
    ang_r = jnp.concatenate([ang_r, ang_r, jnp.zeros((n, LANE - MLA_ROPE), F32)], axis=1)
    ang_k = p * inv_k
    tm = min(n, 1024)
    spec = pl.BlockSpec((tm, LANE), lambda i: (i, 0))
    out = jax.ShapeDtypeStruct((n, LANE), F32)
    return pl.pallas_call(
        _rope_table_kernel,
        grid=(n // tm,),
        in_specs=[spec, spec],
        out_specs=[spec] * 4,
        out_shape=[out] * 4,
        compiler_params=_params("parallel"),
        name="rope_tables",
    )(ang_r, ang_k)


def _ffn_kernel(*refs, cast_next):
    if cast_next:
        x_ref, gpre_ref, wa_ref, wb_ref, wo_ref, gpost_ref, nwi_ref, nwo_ref, o_ref, cwi_ref, cwo_ref, h_ref = refs
    else:
        x_ref, gpre_ref, wa_ref, wb_ref, wo_ref, gpost_ref, o_ref, h_ref = refs
    f = pl.program_id(1)

    def hidden_tile(h):
        a = jnp.dot(h, wa_ref[...], preferred_element_type=F32)
        b = jnp.dot(h, wb_ref[...], preferred_element_type=F32)
        act = (a * jax.nn.sigmoid(a) * b).astype(BF16)
        return jnp.dot(act, wo_ref[...], preferred_element_type=F32)

    def cast_step():
        if cast_next:
            step = pl.program_id(0) * pl.num_programs(1) + f
            cwi_ref[:, :D_FF] = nwi_ref[...].astype(BF16)
            cwi_ref[:, D_FF:] = jnp.zeros((cwi_ref.shape[0], D_FF_PAD - D_FF), BF16)
            cwo_ref[...] = jnp.where(step < D_FF // CAST_ROWS, nwo_ref[...], 0.0).astype(BF16)

    @pl.when(f == 0)
    def _():
        h = _rms(x_ref[...], gpre_ref[...]).astype(BF16)
        h_ref[...] = h
        o_ref[...] = hidden_tile(h)
        cast_step()

    @pl.when(f > 0)
    def _():
        o_ref[...] += hidden_tile(h_ref[...])
        cast_step()

    @pl.when(f == pl.num_programs(1) - 1)
    def _():
        o_ref[...] = x_ref[...] + _rms(o_ref[...], 0.5 * gpost_ref[...])


def _ffn(x, g_pre, wi, wo, g_post, nxt=None, tm=1024):
    n, d = x.shape
    nf = D_FF_PAD // FF_TILE
    tm = min(tm, n)
    steps = (n // tm) * nf
    rows_per_half = d // CAST_ROWS
    n_wi, n_wo_real, n_wo = 2 * rows_per_half, D_FF // CAST_ROWS, D_FF_PAD // CAST_ROWS
    cast_next = nxt is not None and steps >= max(n_wi, n_wo)
    in_specs = [
        pl.BlockSpec((tm, d), lambda i, f: (i, 0)),
        pl.BlockSpec((1, d), lambda i, f: (0, 0)),
        pl.BlockSpec((None, None, d, FF_TILE), lambda i, f: (0, 0, 0, f)),
        pl.BlockSpec((None, None, d, FF_TILE), lambda i, f: (0, 1, 0, f)),
        pl.BlockSpec((None, FF_TILE, d), lambda i, f: (0, f, 0)),
        pl.BlockSpec((1, d), lambda i, f: (0, 0)),
    ]
    out_specs = [pl.BlockSpec((tm, d), lambda i, f: (i, 0))]
    out_shape = [jax.ShapeDtypeStruct((n, d), F32)]
    args = [x, g_pre, wi, wi, wo, g_post]
    if cast_next:
        nwi, nwo, ln = nxt
        wi_blk = lambda i, f: jnp.minimum(i * nf + f, n_wi - 1)
        in_specs += [
            pl.BlockSpec((None, CAST_ROWS, D_FF), lambda i, f: (ln, wi_blk(i, f) % rows_per_half, wi_blk(i, f) // rows_per_half)),
            pl.BlockSpec((None, CAST_ROWS, d), lambda i, f: (ln, jnp.minimum(i * nf + f, n_wo_real - 1), 0)),
        ]
        out_specs += [
            pl.BlockSpec((None, None, CAST_ROWS, D_FF_PAD),
                         lambda i, f: (0, wi_blk(i, f) // rows_per_half, wi_blk(i, f) % rows_per_half, 0)),
            pl.BlockSpec((None, CAST_ROWS, d), lambda i, f: (0, jnp.minimum(i * nf + f, n_wo - 1), 0)),
        ]
        out_shape += [jax.ShapeDtypeStruct((1, 2, d, D_FF_PAD), BF16), jax.ShapeDtypeStruct((1, D_FF_PAD, d), BF16)]
        args += [nwi, nwo]
    out = pl.pallas_call(
        functools.partial(_ffn_kernel, cast_next=cast_next),
        grid=(n // tm, nf),
        in_specs=in_specs,
        out_specs=out_specs,
        out_shape=out_shape,
        scratch_shapes=[pltpu.VMEM((tm, d), BF16)],
        compiler_params=_params("arbitrary" if cast_next else "parallel", "arbitrary",
                                vmem=VMEM_LIMIT_FFN_CAST if cast_next else VMEM_LIMIT),
        name="ffn",
    )(*args)
    if nxt is None:
        return out[0], None
    if cast_next:
        return out[0], (out[1], out[2])
    return out[0], (_cast_wi(nxt[0], nxt[2]), _cast_wo(nxt[1], nxt[2]))


def _norm_mm_kernel(x_ref, g_ref, w_ref, o_ref, h_ref):
    h = _rms(x_ref[...], g_ref[...]).astype(BF16)
    h_ref[...] = h
    o_ref[...] = jnp.dot(h, w_ref[...], preferred_element_type=F32).astype(o_ref.dtype)


def _norm_mm(x, g, w, l, tm=1024):
    n, k = x.shape
    nc = w.shape[2]
    tm = min(tm, n)
    return pl.pallas_call(
        _norm_mm_kernel,
        grid=(n // tm,),
        in_specs=[pl.BlockSpec((tm, k), lambda i: (i, 0)), pl.BlockSpec((1, k), lambda i: (0, 0)),
                  pl.BlockSpec((None, k, nc), lambda i: (l, 0, 0))],
        out_specs=[pl.BlockSpec((tm, nc), lambda i: (i, 0)), pl.BlockSpec((tm, k), lambda i: (i, 0))],
        out_shape=[jax.ShapeDtypeStruct((n, nc), BF16), jax.ShapeDtypeStruct((n, k), BF16)],
        compiler_params=_params("parallel"),
        name="norm_in_proj",
    )(x, g, w)


def _mm_kernel(a_ref, w_ref, o_ref):
    o_ref[...] = jnp.dot(a_ref[...], w_ref[...], preferred_element_type=F32).astype(o_ref.dtype)


def _mm(a, w, l, col0, nc, tn=1024, tm=2048):
    n, k = a.shape
    j0 = col0 // tn
    tm = min(tm, n)
    return pl.pallas_call(
        _mm_kernel,
        grid=(n // tm, nc // tn),
        in_specs=[pl.BlockSpec((tm, k), lambda i, j: (i, 0)), pl.BlockSpec((None, k, tn), lambda i, j: (l, 0, j0 + j))],
        out_specs=pl.BlockSpec((tm, tn), lambda i, j: (i, j)),
        out_shape=jax.ShapeDtypeStruct((n, nc), BF16),
        compiler_params=_params("parallel", "arbitrary"),
        name="in_proj",
    )(a, w)


def _rope_half_block(blk, c, s):
    return blk * c + pltpu.roll(blk, MLA_ROPE, axis=1) * s


def _mla_prep_kernel(z_ref, gq_ref, gkv_ref, wq_ref, wkv_ref, c_ref, s_ref, q_ref, kv_ref, kr_ref):
    c = c_ref[...]
    s = s_ref[...]
    hq = _rms(z_ref[:, :Q_LORA].astype(F32), gq_ref[...]).astype(BF16)
    for h in range(MLA_HEADS):
        lo = h * MLA_QK_PAD
        qh = jnp.dot(hq, wq_ref[:, lo:lo + MLA_QK_PAD], preferred_element_type=F32)
        q_ref[:, lo:lo + LANE] = qh[:, :LANE].astype(BF16)
        q_ref[:, lo + LANE:lo + MLA_QK_PAD] = _rope_half_block(qh[:, LANE:], c, s).astype(BF16)
    hkv = _rms(z_ref[:, Q_LORA:Q_LORA + KV_LORA].astype(F32), gkv_ref[...]).astype(BF16)
    step = 4 * (MLA_NOPE + MLA_V)
    for lo in range(0, MLA_HEADS * (MLA_NOPE + MLA_V), step):
        kv_ref[:, lo:lo + step] = jnp.dot(hkv, wkv_ref[:, lo:lo + step], preferred_element_type=F32).astype(BF16)
    kr_ref[...] = _rope_half_block(z_ref[:, Q_LORA + KV_LORA:].astype(F32), c, s).astype(BF16)


def _mla_prep(zm, gq, gkv, wq, wkv, c_r, s_r, l, tm=512):
    n = zm.shape[0]
    tm = min(tm, n)
    wq_w = MLA_HEADS * MLA_QK_PAD
    wkv_w = MLA_HEADS * (MLA_NOPE + MLA_V)
    row = lambda w: pl.BlockSpec((tm, w), lambda i: (i, 0))
    full = lambda r, w: pl.BlockSpec((r, w), lambda i: (0, 0))
    layer = lambda r, w: pl.BlockSpec((None, r, w), lambda i: (l, 0, 0))
    return pl.pallas_call(
        _mla_prep_kernel,
        grid=(n // tm,),
        in_specs=[row(MLA_GROUP_W), full(1, Q_LORA), full(1, KV_LORA), layer(Q_LORA, wq_w), layer(KV_LORA, wkv_w),
                  row(LANE), row(LANE)],
        out_specs=[row(wq_w), row(wkv_w), row(LANE)],
        out_shape=[jax.ShapeDtypeStruct((n, wq_w), BF16), jax.ShapeDtypeStruct((n, wkv_w), BF16),
                   jax.ShapeDtypeStruct((n, LANE), BF16)],
        compiler_params=_params("parallel"),
        name="mla_prep",
    )(zm, gq, gkv, wq, wkv, c_r, s_r)


def _attn_kernel(q_ref, kv_ref, kr_ref, o_ref, *, tq, tk, hp):
    i = pl.program_id(2)
    c = (MLA_NOPE + MLA_ROPE) ** -0.5 * np.log2(np.e)
    hw = MLA_NOPE + MLA_V

    def tile(k0, carry, width, masked_from):
        kr = kr_ref[pl.ds(k0, width), :]
        if masked_from is not None:
            mw = width - masked_from
            qc = lax.broadcasted_iota(jnp.int32, (tq, mw), 0) // CHUNK
            kc = lax.broadcasted_iota(jnp.int32, (tq, mw), 1) // CHUNK
            mask = kc <= qc
        new = []
        for j in range(hp):
            m, l, acc = carry[j]
            q = q_ref[:, j * MLA_QK_PAD:(j + 1) * MLA_QK_PAD]
            k = jnp.concatenate([kv_ref[pl.ds(k0, width), j * hw:j * hw + MLA_NOPE], kr], axis=1)
            s = lax.dot_general(q, k, (((1,), (1,)), ((), ())), preferred_element_type=F32)
            if masked_from is not None:
                tail = jnp.where(mask, s[:, masked_from:], -1e30)
                s = tail if masked_from == 0 else jnp.concatenate([s[:, :masked_from], tail], axis=1)
            m_new = jnp.maximum(m, jnp.max(s, axis=-1, keepdims=True))
            alpha = jnp.exp2((m - m_new) * c)
            p = jnp.exp2((s - m_new) * c)
            l = alpha * l + jnp.sum(p, axis=-1, keepdims=True)
            v = kv_ref[pl.ds(k0, width), j * hw + MLA_NOPE:(j + 1) * hw]
            acc = alpha * acc + jnp.dot(p.astype(BF16), v, preferred_element_type=F32)
            new.append((m_new, l, acc))
        return tuple(new)

    def body(kb, carry):
        return tile(pl.multiple_of(kb * tk, tk), carry, tk, None)

    def finish(carry):
        for j in range(hp):
            m, l, acc = carry[j]
            o_ref[:, j * MLA_V:(j + 1) * MLA_V] = (acc / l).astype(BF16)

    carry = tuple((jnp.full((tq, 1), -1e30, F32), jnp.zeros((tq, 1), F32), jnp.zeros((tq, MLA_V), F32))
                  for _ in range(hp))
    q0 = i * tq
    n_full = q0 // tk
    carry = lax.fori_loop(0, n_full, body, carry)
    for lead in range(0, tk, tq):
        @pl.when(q0 - n_full * tk == lead)
        def _():
            finish(tile(pl.multiple_of(q0 - lead, tq), carry, lead + tq, lead))


def _attention(q, kv, kr, b, s, tq=512, tk=2048, hp=4):
    n = q.shape[0]
    tq = min(tq, s)
    tk = min(tk, s)
    assert tk % tq == 0
    nq = s // tq
    kv3 = kv.reshape(b, s, kv.shape[1])
    kr3 = kr.reshape(b, s, LANE)
    return pl.pallas_call(
        functools.partial(_attn_kernel, tq=tq, tk=tk, hp=hp),
        grid=(b, MLA_HEADS // hp, nq),
        in_specs=[
            pl.BlockSpec((tq, hp * MLA_QK_PAD), lambda bi, h, i: (bi * nq + i, h)),
            pl.BlockSpec((None, s, hp * (MLA_NOPE + MLA_V)), lambda bi, h, i: (bi, 0, h)),
            pl.BlockSpec((None, s, LANE), lambda bi, h, i: (bi, 0, 0)),
        ],
        out_specs=pl.BlockSpec((tq, hp * MLA_V), lambda bi, h, i: (bi * nq + i, h)),
        out_shape=jax.ShapeDtypeStruct((n, MLA_HEADS * MLA_V), BF16),
        compiler_params=_params("parallel", "parallel", "arbitrary"),
        name="mla_attention",
    )(q, kv3, kr3)


def _ret_kernel(lg_ref, q_ref, k_ref, v_ref, g_ref, c_ref, s_ref, o_ref, state_ref, dec_ref, *, t, hp):
    first = pl.program_id(2) == 0
    c = c_ref[...]
    s = s_ref[...]
    half = RET_QK // 2
    pos = lax.broadcasted_iota(jnp.int32, (t, 1), 0).astype(F32)

    def rope(x):
        x1 = x[:, :half]
        x2 = x[:, half:]
        return jnp.concatenate([x1 * c - x2 * s, x2 * c + x1 * s], axis=1)

    for j in range(hp):
        lg = lg_ref[pl.program_id(1) * hp + j]
        cols = slice(j * RET_QK, (j + 1) * RET_QK)

        @pl.when(first)
        def _():
            state_ref[j] = jnp.zeros((RET_QK, RET_V), F32)
            ii = lax.broadcasted_iota(jnp.int32, (t, t), 0)
            jj = lax.broadcasted_iota(jnp.int32, (t, t), 1)
            dec_ref[j] = jnp.where(jj // CHUNK <= ii // CHUNK, jnp.exp(jnp.abs(ii - jj).astype(F32) * lg), 0.0)

        q = rope(q_ref[:, cols].astype(F32))
        k = rope(k_ref[:, cols].astype(F32)) * RET_QK ** -0.5
        v = v_ref[:, cols]
        q_dec = q * jnp.exp((pos + 1.0) * lg)
        k_dec = k * jnp.exp((t - 1.0 - pos) * lg)
        a = lax.dot_general(q.astype(BF16), k.astype(BF16), (((1,), (1,)), ((), ())),
                            preferred_element_type=F32) * dec_ref[j]
        state = state_ref[j]
        o = jnp.dot(a.astype(BF16), v, preferred_element_type=F32)
        o = o + jnp.dot(q_dec.astype(BF16), state.astype(BF16), preferred_element_type=F32)
        block_decay = jnp.exp(jnp.full((1, RET_V), t * 1.0, F32) * lg)
        state_ref[j] = state * block_decay + lax.dot_general(
            k_dec.astype(BF16), v, (((0,), (0,)), ((), ())), preferred_element_type=F32)

        mu = jnp.mean(o, axis=-1, keepdims=True)
        d = o - mu
        var = jnp.mean(d * d, axis=-1, keepdims=True)
        on = d * lax.rsqrt(var + RET_GN_EPS)
        g = g_ref[:, cols].astype(F32)
        o_ref[:, cols] = (g * jax.nn.sigmoid(g) * on).astype(BF16)


def _retention(zr, c_k, s_k, b, s, t=512, hp=2):
    n = zr.shape[0]
    t = min(t, s)
    nt = s // t
    ng = RET_HEADS // hp
    log_g = jnp.log1p(-(2.0 ** (-5.0 - jnp.arange(RET_HEADS, dtype=F32))))
    col = lambda part: pl.BlockSpec((t, hp * RET_QK), lambda bi, h, ti: (bi * nt + ti, part * ng + h))
    tab = pl.BlockSpec((t, LANE), lambda bi, h, ti: (bi * nt + ti, 0))
    return pl.pallas_call(
        functools.partial(_ret_kernel, t=t, hp=hp),
        grid=(b, ng, nt),
        in_specs=[pl.BlockSpec(memory_space=pltpu.SMEM), col(0), col(1), col(2), col(3), tab, tab],
        out_specs=pl.BlockSpec((t, hp * RET_V), lambda bi, h, ti: (bi * nt + ti, h)),
        out_shape=jax.ShapeDtypeStruct((n, RET_HEADS * RET_V), BF16),
        scratch_shapes=[pltpu.VMEM((hp, RET_QK, RET_V), F32), pltpu.VMEM((hp, t, t), F32)],
        compiler_params=_params("parallel", "parallel", "arbitrary"),
        name="retention",
    )(log_g, zr, zr, zr, zr, c_k, s_k)


def _gmlp_kernel(u_ref, v_ref, lng_ref, lnb_ref, ws_ref, bst_ref, o_ref, *, nblk):
    v = jax.nn.gelu(v_ref[...].astype(F32))
    mu = jnp.mean(v, axis=-1, keepdims=True)
    d = v - mu
    var = jnp.mean(d * d, axis=-1, keepdims=True)
    vn = (d * lax.rsqrt(var + EPS) * lng_ref[...] + lnb_ref[...]).astype(BF16)
    pc_i = lax.broadcasted_iota(jnp.int32, (GM_BLOCK, GM_BLOCK), 0) // CHUNK
    pc_j = lax.broadcasted_iota(jnp.int32, (GM_BLOCK, GM_BLOCK), 1) // CHUNK
    gw = GM_WIDTH // GM_GROUPS
    for g in range(GM_GROUPS):
        w = jnp.where(pc_i >= pc_j, ws_ref[g], 0.0).astype(BF16)
        bias = bst_ref[:, g:g + 1]
        for r in range(nblk):
            rows = slice(r * GM_BLOCK, (r + 1) * GM_BLOCK)
            cols = slice(g * gw, (g + 1) * gw)
            mixed = jnp.dot(w, vn[rows, cols], preferred_element_type=F32) + bias
            u = jax.nn.gelu(u_ref[rows, cols].astype(F32))
            o_ref[rows, cols] = (u * mixed).astype(BF16)


def _gmlp(zg, ln_g, ln_b, w_s, b_s_t, l, nblk=2):
    n = zg.shape[0]
    tm = nblk * GM_BLOCK
    return pl.pallas_call(
        functools.partial(_gmlp_kernel, nblk=nblk),
        grid=(n // tm,),
        in_specs=[
            pl.BlockSpec((tm, GM_WIDTH), lambda i: (i, 0)),
            pl.BlockSpec((tm, GM_WIDTH), lambda i: (i, 1)),
            pl.BlockSpec((1, GM_WIDTH), lambda i: (0, 0)),
            pl.BlockSpec((1, GM_WIDTH), lambda i: (0, 0)),
            pl.BlockSpec((None, GM_GROUPS, GM_BLOCK, GM_BLOCK), lambda i: (l, 0, 0, 0)),
            pl.BlockSpec((GM_BLOCK, GM_GROUPS), lambda i: (0, 0)),
        ],
        out_specs=pl.BlockSpec((tm, GM_WIDTH), lambda i: (i, 0)),
        out_shape=jax.ShapeDtypeStruct((n, GM_WIDTH), BF16),
        compiler_params=_params("parallel"),
        name="gmlp",
    )(zg, zg, ln_g, ln_b, w_s, b_s_t)


def _merge_kernel(h_ref, ya_ref, yb_ref, yc_ref, wg0_ref, wg1_ref, wg2_ref, bg0_ref, bg1_ref, bg2_ref,
                  wb0_ref, wb1_ref, wb2_ref, o_ref):
    h = h_ref[...]

    def branch(y_ref, wg_ref, bg_ref, wb_ref):
        gate = jax.nn.sigmoid(jnp.dot(h, wg_ref[...], preferred_element_type=F32) + bg_ref[...])
        return gate * jnp.dot(y_ref[...], wb_ref[...], preferred_element_type=F32)

    merged = branch(ya_ref, wg0_ref, bg0_ref, wb0_ref)
    merged = merged + branch(yb_ref, wg1_ref, bg1_ref, wb1_ref)
    merged = merged + branch(yc_ref, wg2_ref, bg2_ref, wb2_ref)
    o_ref[...] = merged.astype(BF16)


def _merge(h, ya, yb, yc, w_gate, b_gate, w_br, l, tm=512, tn=512):
    n, d = h.shape
    tm = min(tm, n)
    nj = d // tn
    act = pl.BlockSpec((tm, d), lambda i, j: (i, 0))
    wg = lambda br: pl.BlockSpec((None, d, tn), lambda i, j: (l, 0, br * nj + j))
    bg = lambda br: pl.BlockSpec((1, tn), lambda i, j: (0, br * nj + j))
    wb = lambda br: pl.BlockSpec((None, None, d, tn), lambda i, j: (l, br, 0, j))
    return pl.pallas_call(
        _merge_kernel,
        grid=(n // tm, nj),
        in_specs=[act, act, act, act, wg(0), wg(1), wg(2), bg(0), bg(1), bg(2), wb(0), wb(1), wb(2)],
        out_specs=pl.BlockSpec((tm, tn), lambda i, j: (i, j)),
        out_shape=jax.ShapeDtypeStruct((n, d), BF16),
        compiler_params=_params("parallel", "arbitrary"),
        name="merge",
    )(h, ya, yb, yc, w_gate, w_gate, w_gate, b_gate, b_gate, b_gate, w_br, w_br, w_br)


def _out_proj_kernel(m_ref, w_ref, g_ref, x_ref, o_ref):
    y = jnp.dot(m_ref[...], w_ref[...], preferred_element_type=F32)
    o_ref[...] = x_ref[...] + _rms(y, g_ref[...])


def _out_proj(merged, w_o, g_post, x, l, tm=512):
    n, d = x.shape
    tm = min(tm, n)
    return pl.pallas_call(
        _out_proj_kernel,
        grid=(n // tm,),
        in_specs=[
            pl.BlockSpec((tm, d), lambda i: (i, 0)),
            pl.BlockSpec((None, d, d), lambda i: (l, 0, 0)),
            pl.BlockSpec((1, d), lambda i: (0, 0)),
            pl.BlockSpec((tm, d), lambda i: (i, 0)),
        ],
        out_specs=pl.BlockSpec((tm, d), lambda i: (i, 0)),
        out_shape=jax.ShapeDtypeStruct((n, d), F32),
        compiler_params=_params("parallel"),
        name="out_proj",
    )(merged, w_o, g_post, x)


def _rot_cols(w):
    half = w.shape[-1] // 2
    return jnp.concatenate([-w[..., half:], w[..., :half]], axis=-1)


def _cast_wi_kernel(w_ref, o_ref):
    o_ref[:, :D_FF] = w_ref[...].astype(BF16)
    o_ref[:, D_FF:] = jnp.zeros((o_ref.shape[0], D_FF_PAD - D_FF), BF16)


def _cast_wi(wi, layer, tr=256):
    d = wi.shape[1]
    return pl.pallas_call(
        _cast_wi_kernel,
        grid=(2, d // tr),
        in_specs=[pl.BlockSpec((None, tr, D_FF), lambda h, r: (layer, r, h))],
        out_specs=pl.BlockSpec((None, None, tr, D_FF_PAD), lambda h, r: (0, h, r, 0)),
        out_shape=jax.ShapeDtypeStruct((1, 2, d, D_FF_PAD), BF16),
        compiler_params=_params("parallel", "parallel"),
        name="cast_wi",
    )(wi)


def _cast_wo_kernel(w_ref, o_ref):
    rows = pl.program_id(0) * FF_TILE + lax.broadcasted_iota(jnp.int32, (FF_TILE, 1), 0)
    o_ref[...] = jnp.where(rows < D_FF, w_ref[...], 0.0).astype(BF16)


def _cast_wo(wo, layer):
    d = wo.shape[2]
    return pl.pallas_call(
        _cast_wo_kernel,
        grid=(D_FF_PAD // FF_TILE,),
        in_specs=[pl.BlockSpec((None, FF_TILE, d), lambda j: (layer, j, 0))],
        out_specs=pl.BlockSpec((None, FF_TILE, d), lambda j: (0, j, 0)),
        out_shape=jax.ShapeDtypeStruct((1, D_FF_PAD, d), BF16),
        compiler_params=_params("parallel"),
        name="cast_wo",
    )(wo)


W_IN_SHIFT = MLA_ROPE
W_IN_TILE = 1024


def _cast_w_in_kernel(a_ref, b_ref, o_ref):
    o_ref[...] = jnp.concatenate([a_ref[:, W_IN_SHIFT:], b_ref[:, :W_IN_SHIFT]], axis=1).astype(BF16)


def _cast_w_in(w_in, tr=1024):
    nl, d, _ = w_in.shape
    first = (Q_LORA + KV_LORA) // W_IN_TILE
    per_tile = W_IN_TILE // LANE
    return pl.pallas_call(
        _cast_w_in_kernel,
        grid=(nl, d // tr, (RET_GROUP_W + GM_GROUP_W) // W_IN_TILE),
        in_specs=[pl.BlockSpec((None, tr, W_IN_TILE), lambda l, r, j: (l, r, first + j)),
                  pl.BlockSpec((None, tr, LANE), lambda l, r, j: (l, r, (first + j + 1) * per_tile))],
        out_specs=pl.BlockSpec((None, tr, W_IN_TILE), lambda l, r, j: (l, r, j)),
        out_shape=jax.ShapeDtypeStruct((nl, d, RET_GROUP_W + GM_GROUP_W), BF16),
        compiler_params=_params("parallel", "parallel", "parallel"),
        name="cast_w_in",
    )(w_in, w_in)


def _cast_w_mla_kernel(a_ref, b_ref, o_ref):
    o_kr = Q_LORA + KV_LORA
    o_ref[:, :o_kr] = a_ref[...].astype(BF16)
    w_kr = b_ref[:, :MLA_ROPE]
    o_ref[:, o_kr:] = jnp.concatenate([w_kr, _rot_cols(w_kr)], axis=1).astype(BF16)


def _cast_w_mla(w_in, tr=512):
    nl, d, _ = w_in.shape
    o_kr = Q_LORA + KV_LORA
    return pl.pallas_call(
        _cast_w_mla_kernel,
        grid=(nl, d // tr),
        in_specs=[pl.BlockSpec((None, tr, o_kr), lambda l, r: (l, r, 0)),
                  pl.BlockSpec((None, tr, LANE), lambda l, r: (l, r, o_kr // LANE))],
        out_specs=pl.BlockSpec((None, tr, MLA_GROUP_W), lambda l, r: (l, r, 0)),
        out_shape=jax.ShapeDtypeStruct((nl, d, MLA_GROUP_W), BF16),
        compiler_params=_params("parallel", "parallel"),
        name="cast_w_mla",
    )(w_in, w_in)


def _prep_w_uq(w_uq):
    nl = w_uq.shape[0]
    w = w_uq.reshape(nl, Q_LORA, MLA_HEADS, MLA_NOPE + MLA_ROPE)
    w_rope = w[..., MLA_NOPE:]
    w = jnp.concatenate([w, _rot_cols(w_rope)], axis=-1)
    return w.reshape(nl, Q_LORA, MLA_HEADS * MLA_QK_PAD).astype(BF16)


def _prep_weights(p):
    w = {}
    w_in = p["w_in"].astype(BF16)
    w["w_mla"] = _cast_w_mla(w_in)
    w["w_rg"] = _cast_w_in(w_in)
    w["w_uq"] = _prep_w_uq(p["w_uq"])
    for name in ("w_ukv", "w_gate", "w_br", "w_o"):
        w[name] = p[name].astype(BF16)
    w["gm_b_s_t"] = jnp.swapaxes(p["gm_b_s"], 1, 2)
    return w


def _row(v):
    return v.reshape(1, -1)


def _token_mixer(x, l, p, w, tables, b, s):
    c_r, s_r, c_k, s_k = tables
    zm, h = _norm_mm(x, _row(p["mix_pre_g"][l]), w["w_mla"], l)
    zr = _mm(h, w["w_rg"], l, 0, RET_GROUP_W)
    zg = _mm(h, w["w_rg"], l, RET_GROUP_W, GM_GROUP_W)
    q, kv, kr = _mla_prep(zm, _row(p["q_norm_g"][l]), _row(p["kv_norm_g"][l]), w["w_uq"], w["w_ukv"], c_r, s_r, l)
    y_a = _attention(q, kv, kr, b, s)
    y_b = _retention(zr, c_k, s_k, b, s)
    y_c = _gmlp(zg, _row(p["gm_ln_g"][l]), _row(p["gm_ln_b"][l]), p["gm_w_s"], w["gm_b_s_t"][l], l)
    merged = _merge(h, y_a, y_b, y_c, w["w_gate"], _row(p["b_gate"][l]), w["w_br"], l)
    return _out_proj(merged, w["w_o"], _row(p["mix_post_g"][l]), x, l)


def _trunk(x, pos, p, depth):
    b, s, d = x.shape
    tables = _rope_tables(pos)
    w = _prep_weights(p)
    x = x.reshape(b * s, d)
    w_ffn = (_cast_wi(p["ffn1_wi"], 0), _cast_wo(p["ffn1_wo"], 0))
    for l in range(depth):
        x, w_ffn = _ffn(x, _row(p["ffn1_pre_g"][l]), *w_ffn, _row(p["ffn1_post_g"][l]),
                        nxt=(p["ffn2_wi"], p["ffn2_wo"], l))
        x = _token_mixer(x, l, p, w, tables, b, s)
        nxt = (p["ffn1_wi"], p["ffn1_wo"], l + 1) if l + 1 < depth else None
        x, w_ffn = _ffn(x, _row(p["ffn2_pre_g"][l]), *w_ffn, _row(p["ffn2_post_g"][l]), nxt=nxt)
    return x.reshape(b, s, d)


def kernel(x, pos, ffn1_pre_g, ffn1_wi, ffn1_wo, ffn1_post_g, mix_pre_g, w_in, q_norm_g, w_uq, kv_norm_g, w_ukv, gm_ln_g, gm_ln_b, gm_w_s, gm_b_s, w_gate, b_gate, w_br, w_o, mix_post_g, ffn2_pre_g, ffn2_wi, ffn2_wo, ffn2_post_g):
    p = dict(ffn1_pre_g=ffn1_pre_g, ffn1_wi=ffn1_wi, ffn1_wo=ffn1_wo, ffn1_post_g=ffn1_post_g, mix_pre_g=mix_pre_g,
             w_in=w_in, q_norm_g=q_norm_g, w_uq=w_uq, kv_norm_g=kv_norm_g, w_ukv=w_ukv, gm_ln_g=gm_ln_g,
             gm_ln_b=gm_ln_b, gm_w_s=gm_w_s, gm_b_s=gm_b_s, w_gate=w_gate, b_gate=b_gate, w_br=w_br, w_o=w_o,
             mix_post_g=mix_post_g, ffn2_pre_g=ffn2_pre_g, ffn2_wi=ffn2_wi, ffn2_wo=ffn2_wo, ffn2_post_g=ffn2_post_g)
    return _trunk(x, pos, p, DEPTH)
```

```python
import functools

import numpy as np
import jax
import jax.numpy as jnp
from jax import lax
from jax.experimental import pallas as pl
from jax.experimental.pallas import tpu as pltpu

F32 = jnp.float32
BF16 = jnp.bfloat16

D_MODEL = 2048
DEPTH = 4
CHUNK = 64
EPS = 1e-6
ROPE_BASE = 10000.0
MLA_HEADS = 16
MLA_NOPE = 128
MLA_ROPE = 64
MLA_V = 128
Q_LORA = 512
KV_LORA = 512
RET_HEADS = 8
RET_QK = 256
RET_V = 256
RET_GN_EPS = 1e-5
GM_GROUPS = 4
GM_WIDTH = 2048
GM_BLOCK = 128
N_BRANCH = 3
D_FF = 5504

LANE = 128
MLA_QK_PAD = 2 * LANE
MLA_GROUP_W = Q_LORA + KV_LORA + LANE
RET_GROUP_W = 4 * RET_HEADS * RET_QK
GM_GROUP_W = 2 * GM_WIDTH
FF_TILE = 512
CAST_ROWS = 32
D_FF_PAD = ((D_FF + FF_TILE - 1) // FF_TILE) * FF_TILE
VMEM_LIMIT = 56 * 2 ** 20
VMEM_LIMIT_FFN_CAST = 60 * 2 ** 20


def _params(*sem, vmem=VMEM_LIMIT):
    return pltpu.CompilerParams(dimension_semantics=sem, vmem_limit_bytes=vmem)


def _rms(x, g):
    return x * lax.rsqrt(jnp.mean(x * x, axis=-1, keepdims=True) + EPS) * g


def _rope_table_kernel(ang_r_ref, ang_k_ref, cr_ref, sr_ref, ck_ref, sk_ref):
    a = ang_r_ref[...]
    live = lax.broadcasted_iota(jnp.int32, a.shape, 1) < MLA_ROPE
    cr_ref[...] = jnp.where(live, jnp.cos(a), 0.0)
    sr_ref[...] = jnp.where(live, jnp.sin(a), 0.0)
    k = ang_k_ref[...]
    ck_ref[...] = jnp.cos(k)
    sk_ref[...] = jnp.sin(k)


def _rope_tables(pos):
    n = pos.size
    p = pos.astype(F32).reshape(n, 1)
    inv_r = ROPE_BASE ** (-jnp.arange(0, MLA_ROPE, 2, dtype=F32) / MLA_ROPE)
    inv_k = ROPE_BASE ** (-jnp.arange(0, RET_QK, 2, dtype=F32) / RET_QK)
    ang_r = p * inv_r
    ang_r = jnp.concatenate([ang_r, ang_r, jnp.zeros((n, LANE - MLA_ROPE), F32)], axis=1)
    ang_k = p * inv_k
    tm = min(n, 1024)
    spec = pl.BlockSpec((tm, LANE), lambda i: (i, 0))
    out = jax.ShapeDtypeStruct((n, LANE), F32)
    return pl.pallas_call(
        _rope_table_kernel,
        grid=(n // tm,),
        in_specs=[spec, spec],
        out_specs=[spec] * 4,
        out_shape=[out] * 4,
        compiler_params=_params("parallel"),
        name="rope_tables",
    )(ang_r, ang_k)


def _ffn_kernel(*refs, cast_next):
    if cast_next:
        x_ref, gpre_ref, wa_ref, wb_ref, wo_ref, gpost_ref, nwi_ref, nwo_ref, o_ref, cwi_ref, cwo_ref, h_ref = refs
    else:
        x_ref, gpre_ref, wa_ref, wb_ref, wo_ref, gpost_ref, o_ref, h_ref = refs
    f = pl.program_id(1)

    def hidden_tile(h):
        a = jnp.dot(h, wa_ref[...], preferred_element_type=F32)
        b = jnp.dot(h, wb_ref[...], preferred_element_type=F32)
        act = (a * jax.nn.sigmoid(a) * b).astype(BF16)
        return jnp.dot(act, wo_ref[...], preferred_element_type=F32)

    def cast_step():
        if cast_next:
            step = pl.program_id(0) * pl.num_programs(1) + f
            cwi_ref[:, :D_FF] = nwi_ref[...].astype(BF16)
            cwi_ref[:, D_FF:] = jnp.zeros((cwi_ref.shape[0], D_FF_PAD - D_FF), BF16)
            cwo_ref[...] = jnp.where(step < D_FF // CAST_ROWS, nwo_ref[...], 0.0).astype(BF16)

    @pl.when(f == 0)
    def _():
        h = _rms(x_ref[...], gpre_ref[...]).astype(BF16)
        h_ref[...] = h
        o_ref[...] = hidden_tile(h)
        cast_step()

    @pl.when(f > 0)
    def _():
        o_ref[...] += hidden_tile(h_ref[...])
        cast_step()

    @pl.when(f == pl.num_programs(1) - 1)
    def _():
        o_ref[...] = x_ref[...] + _rms(o_ref[...], 0.5 * gpost_ref[...])


def _ffn(x, g_pre, wi, wo, g_post, nxt=None, tm=1024):
    n, d = x.shape
    nf = D_FF_PAD // FF_TILE
    tm = min(tm, n)
    steps = (n // tm) * nf
    rows_per_half = d // CAST_ROWS
    n_wi, n_wo_real, n_wo = 2 * rows_per_half, D_FF // CAST_ROWS, D_FF_PAD // CAST_ROWS
    cast_next = nxt is not None and steps >= max(n_wi, n_wo)
    in_specs = [
        pl.BlockSpec((tm, d), lambda i, f: (i, 0)),
        pl.BlockSpec((1, d), lambda i, f: (0, 0)),
        pl.BlockSpec((None, None, d, FF_TILE), lambda i, f: (0, 0, 0, f)),
        pl.BlockSpec((None, None, d, FF_TILE), lambda i, f: (0, 1, 0, f)),
        pl.BlockSpec((None, FF_TILE, d), lambda i, f: (0, f, 0)),
        pl.BlockSpec((1, d), lambda i, f: (0, 0)),
    ]
    out_specs = [pl.BlockSpec((tm, d), lambda i, f: (i, 0))]
    out_shape = [jax.ShapeDtypeStruct((n, d), F32)]
    args = [x, g_pre, wi, wi, wo, g_post]
    if cast_next:
        nwi, nwo, ln = nxt
        wi_blk = lambda i, f: jnp.minimum(i * nf + f, n_wi - 1)
        in_specs += [
            pl.BlockSpec((None, CAST_ROWS, D_FF), lambda i, f: (ln, wi_blk(i, f) % rows_per_half, wi_blk(i, f) // rows_per_half)),
            pl.BlockSpec((None, CAST_ROWS, d), lambda i, f: (ln, jnp.minimum(i * nf + f, n_wo_real - 1), 0)),
        ]
        out_specs += [
            pl.BlockSpec((None, None, CAST_ROWS, D_FF_PAD),
                         lambda i, f: (0, wi_blk(i, f) // rows_per_half, wi_blk(i, f) % rows_per_half, 0)),
            pl.BlockSpec((None, CAST_ROWS, d), lambda i, f: (0, jnp.minimum(i * nf + f, n_wo - 1), 0)),
        ]
        out_shape += [jax.ShapeDtypeStruct((1, 2, d, D_FF_PAD), BF16), jax.ShapeDtypeStruct((1, D_FF_PAD, d), BF16)]
        args += [nwi, nwo]
    out = pl.pallas_call(
        functools.partial(_ffn_kernel, cast_next=cast_next),
        grid=(n // tm, nf),
        in_specs=in_specs,
        out_specs=out_specs,
        out_shape=out_shape,
        scratch_shapes=[pltpu.VMEM((tm, d), BF16)],
        compiler_params=_params("arbitrary" if cast_next else "parallel", "arbitrary",
                                vmem=VMEM_LIMIT_FFN_CAST if cast_next else VMEM_LIMIT),
        name="ffn",
    )(*args)
    if nxt is None:
        return out[0], None
    if cast_next:
        return out[0], (out[1], out[2])
    return out[0], (_cast_wi(nxt[0], nxt[2]), _cast_wo(nxt[1], nxt[2]))


def _norm_mm_kernel(x_ref, g_ref, w_ref, o_ref, h_ref):
    h = _rms(x_ref[...], g_ref[...]).astype(BF16)
    h_ref[...] = h
    o_ref[...] = jnp.dot(h, w_ref[...], preferred_element_type=F32).astype(o_ref.dtype)


def _norm_mm(x, g, w, l, tm=1024):
    n, k = x.shape
    nc = w.shape[2]
    tm = min(tm, n)
    return pl.pallas_call(
        _norm_mm_kernel,
        grid=(n // tm,),
        in_specs=[pl.BlockSpec((tm, k), lambda i: (i, 0)), pl.BlockSpec((1, k), lambda i: (0, 0)),
                  pl.BlockSpec((None, k, nc), lambda i: (l, 0, 0))],
        out_specs=[pl.BlockSpec((tm, nc), lambda i: (i, 0)), pl.BlockSpec((tm, k), lambda i: (i, 0))],
        out_shape=[jax.ShapeDtypeStruct((n, nc), BF16), jax.ShapeDtypeStruct((n, k), BF16)],
        compiler_params=_params("parallel"),
        name="norm_in_proj",
    )(x, g, w)


def _mm_kernel(*refs, n_cast):
    a_ref, w_ref = refs[:2]
    o_ref = refs[2 + n_cast]
    o_ref[...] = jnp.dot(a_ref[...], w_ref[...], preferred_element_type=F32).astype(o_ref.dtype)
    for src_ref, dst_ref in zip(refs[2:2 + n_cast], refs[3 + n_cast:]):
        dst_ref[...] = src_ref[...].astype(BF16)


def _mm(a, w, l, col0, nc, carry=(), tn=1024, tm=2048):
    n, k = a.shape
    j0 = col0 // tn
    tm = min(tm, n)
    nj = nc // tn
    steps = (n // tm) * nj
    inside = [(src, rb) for src, rb in carry if src.shape[1] % rb == 0 and src.shape[1] // rb <= steps]
    in_specs = [pl.BlockSpec((tm, k), lambda i, j: (i, 0)), pl.BlockSpec((None, k, tn), lambda i, j: (l, 0, j0 + j))]
    out_specs = [pl.BlockSpec((tm, tn), lambda i, j: (i, j))]
    out_shape = [jax.ShapeDtypeStruct((n, nc), BF16)]
    for src, rb in inside:
        last = src.shape[1] // rb - 1
        blk = lambda i, j, last=last: jnp.minimum(i * nj + j, last)
        in_specs.append(pl.BlockSpec((None, rb, src.shape[2]), lambda i, j, blk=blk: (l, blk(i, j), 0)))
        out_specs.append(pl.BlockSpec((None, rb, src.shape[2]), lambda i, j, blk=blk: (0, blk(i, j), 0)))
        out_shape.append(jax.ShapeDtypeStruct((1,) + src.shape[1:], BF16))
    out = pl.pallas_call(
        functools.partial(_mm_kernel, n_cast=len(inside)),
        grid=(n // tm, nj),
        in_specs=in_specs,
        out_specs=out_specs,
        out_shape=out_shape,
        compiler_params=_params("arbitrary", "arbitrary"),
        name="in_proj",
    )(a, w, *[src for src, _ in inside])
    converted = iter(out[1:])
    done = [next(converted) if any(src is s for s, _ in inside) else src[l:l + 1].astype(BF16) for src, _ in carry]
    return out[0], done


def _rope_half_block(blk, c, s):
    return blk * c + pltpu.roll(blk, MLA_ROPE, axis=1) * s


def _mla_prep_kernel(z_ref, gq_ref, gkv_ref, wq_ref, wkv_ref, c_ref, s_ref, q_ref, kv_ref, kr_ref):
    c = c_ref[...]
    s = s_ref[...]
    hq = _rms(z_ref[:, :Q_LORA].astype(F32), gq_ref[...]).astype(BF16)
    for h in range(MLA_HEADS):
        lo = h * MLA_QK_PAD
        qh = jnp.dot(hq, wq_ref[:, lo:lo + MLA_QK_PAD], preferred_element_type=F32)
        q_ref[:, lo:lo + LANE] = qh[:, :LANE].astype(BF16)
        q_ref[:, lo + LANE:lo + MLA_QK_PAD] = _rope_half_block(qh[:, LANE:], c, s).astype(BF16)
    hkv = _rms(z_ref[:, Q_LORA:Q_LORA + KV_LORA].astype(F32), gkv_ref[...]).astype(BF16)
    step = 4 * (MLA_NOPE + MLA_V)
    for lo in range(0, MLA_HEADS * (MLA_NOPE + MLA_V), step):
        kv_ref[:, lo:lo + step] = jnp.dot(hkv, wkv_ref[:, lo:lo + step], preferred_element_type=F32).astype(BF16)
    kr_ref[...] = _rope_half_block(z_ref[:, Q_LORA + KV_LORA:].astype(F32), c, s).astype(BF16)


def _mla_prep(zm, gq, gkv, wq, wkv, c_r, s_r, l, l_kv, tm=512):
    n = zm.shape[0]
    tm = min(tm, n)
    wq_w = MLA_HEADS * MLA_QK_PAD
    wkv_w = MLA_HEADS * (MLA_NOPE + MLA_V)
    row = lambda w: pl.BlockSpec((tm, w), lambda i: (i, 0))
    full = lambda r, w: pl.BlockSpec((r, w), lambda i: (0, 0))
    layer = lambda r, w, idx: pl.BlockSpec((None, r, w), lambda i: (idx, 0, 0))
    return pl.pallas_call(
        _mla_prep_kernel,
        grid=(n // tm,),
        in_specs=[row(MLA_GROUP_W), full(1, Q_LORA), full(1, KV_LORA), layer(Q_LORA, wq_w, l), layer(KV_LORA, wkv_w, l_kv),
                  row(LANE), row(LANE)],
        out_specs=[row(wq_w), row(wkv_w), row(LANE)],
        out_shape=[jax.ShapeDtypeStruct((n, wq_w), BF16), jax.ShapeDtypeStruct((n, wkv_w), BF16),
                   jax.ShapeDtypeStruct((n, LANE), BF16)],
        compiler_params=_params("parallel"),
        name="mla_prep",
    )(zm, gq, gkv, wq, wkv, c_r, s_r)


def _attn_kernel(q_ref, kv_ref, kr_ref, o_ref, *, tq, tk, hp):
    i = pl.program_id(2)
    c = (MLA_NOPE + MLA_ROPE) ** -0.5 * np.log2(np.e)
    hw = MLA_NOPE + MLA_V

    def tile(k0, carry, width, masked_from):
        kr = kr_ref[pl.ds(k0, width), :]
        if masked_from is not None:
            mw = width - masked_from
            qc = lax.broadcasted_iota(jnp.int32, (tq, mw), 0) // CHUNK
            kc = lax.broadcasted_iota(jnp.int32, (tq, mw), 1) // CHUNK
            mask = kc <= qc
        new = []
        for j in range(hp):
            m, l, acc = carry[j]
            q = q_ref[:, j * MLA_QK_PAD:(j + 1) * MLA_QK_PAD]
            k = jnp.concatenate([kv_ref[pl.ds(k0, width), j * hw:j * hw + MLA_NOPE], kr], axis=1)
            s = lax.dot_general(q, k, (((1,), (1,)), ((), ())), preferred_element_type=F32)
            if masked_from is not None:
                tail = jnp.where(mask, s[:, masked_from:], -1e30)
                s = tail if masked_from == 0 else jnp.concatenate([s[:, :masked_from], tail], axis=1)
            m_new = jnp.maximum(m, jnp.max(s, axis=-1, keepdims=True))
            alpha = jnp.exp2((m - m_new) * c)
            p = jnp.exp2((s - m_new) * c)
            l = alpha * l + jnp.sum(p, axis=-1, keepdims=True)
            v = kv_ref[pl.ds(k0, width), j * hw + MLA_NOPE:(j + 1) * hw]
            acc = alpha * acc + jnp.dot(p.astype(BF16), v, preferred_element_type=F32)
            new.append((m_new, l, acc))
        return tuple(new)

    def body(kb, carry):
        return tile(pl.multiple_of(kb * tk, tk), carry, tk, None)

    def finish(carry):
        for j in range(hp):
            m, l, acc = carry[j]
            o_ref[:, j * MLA_V:(j + 1) * MLA_V] = (acc / l).astype(BF16)

    carry = tuple((jnp.full((tq, 1), -1e30, F32), jnp.zeros((tq, 1), F32), jnp.zeros((tq, MLA_V), F32))
                  for _ in range(hp))
    q0 = i * tq
    n_full = q0 // tk
    carry = lax.fori_loop(0, n_full, body, carry)
    for lead in range(0, tk, tq):
        @pl.when(q0 - n_full * tk == lead)
        def _():
            finish(tile(pl.multiple_of(q0 - lead, tq), carry, lead + tq, lead))


def _attention(q, kv, kr, b, s, tq=512, tk=2048, hp=4):
    n = q.shape[0]
    tq = min(tq, s)
    tk = min(tk, s)
    assert tk % tq == 0
    nq = s // tq
    kv3 = kv.reshape(b, s, kv.shape[1])
    kr3 = kr.reshape(b, s, LANE)
    return pl.pallas_call(
        functools.partial(_attn_kernel, tq=tq, tk=tk, hp=hp),
        grid=(b, MLA_HEADS // hp, nq),
        in_specs=[
            pl.BlockSpec((tq, hp * MLA_QK_PAD), lambda bi, h, i: (bi * nq + i, h)),
            pl.BlockSpec((None, s, hp * (MLA_NOPE + MLA_V)), lambda bi, h, i: (bi, 0, h)),
            pl.BlockSpec((None, s, LANE), lambda bi, h, i: (bi, 0, 0)),
        ],
        out_specs=pl.BlockSpec((tq, hp * MLA_V), lambda bi, h, i: (bi * nq + i, h)),
        out_shape=jax.ShapeDtypeStruct((n, MLA_HEADS * MLA_V), BF16),
        compiler_params=_params("parallel", "parallel", "arbitrary"),
        name="mla_attention",
    )(q, kv3, kr3)


def _ret_kernel(lg_ref, q_ref, k_ref, v_ref, g_ref, c_ref, s_ref, o_ref, state_ref, dec_ref, *, t, hp):
    first = pl.program_id(2) == 0
    c = c_ref[...]
    s = s_ref[...]
    half = RET_QK // 2
    pos = lax.broadcasted_iota(jnp.int32, (t, 1), 0).astype(F32)

    def rope(x):
        x1 = x[:, :half]
        x2 = x[:, half:]
        return jnp.concatenate([x1 * c - x2 * s, x2 * c + x1 * s], axis=1)

    for j in range(hp):
        lg = lg_ref[pl.program_id(1) * hp + j]
        cols = slice(j * RET_QK, (j + 1) * RET_QK)

        @pl.when(first)
        def _():
            state_ref[j] = jnp.zeros((RET_QK, RET_V), F32)
            ii = lax.broadcasted_iota(jnp.int32, (t, t), 0)
            jj = lax.broadcasted_iota(jnp.int32, (t, t), 1)
            dec_ref[j] = jnp.where(jj // CHUNK <= ii // CHUNK, jnp.exp(jnp.abs(ii - jj).astype(F32) * lg), 0.0)

        q = rope(q_ref[:, cols].astype(F32))
        k = rope(k_ref[:, cols].astype(F32)) * RET_QK ** -0.5
        v = v_ref[:, cols]
        q_dec = q * jnp.exp((pos + 1.0) * lg)
        k_dec = k * jnp.exp((t - 1.0 - pos) * lg)
        a = lax.dot_general(q.astype(BF16), k.astype(BF16), (((1,), (1,)), ((), ())),
                            preferred_element_type=F32) * dec_ref[j]
        state = state_ref[j]
        o = jnp.dot(a.astype(BF16), v, preferred_element_type=F32)
        o = o + jnp.dot(q_dec.astype(BF16), state.astype(BF16), preferred_element_type=F32)
        block_decay = jnp.exp(jnp.full((1, RET_V), t * 1.0, F32) * lg)
        state_ref[j] = state * block_decay + lax.dot_general(
            k_dec.astype(BF16), v, (((0,), (0,)), ((), ())), preferred_element_type=F32)

        mu = jnp.mean(o, axis=-1, keepdims=True)
        d = o - mu
        var = jnp.mean(d * d, axis=-1, keepdims=True)
        on = d * lax.rsqrt(var + RET_GN_EPS)
        g = g_ref[:, cols].astype(F32)
        o_ref[:, cols] = (g * jax.nn.sigmoid(g) * on).astype(BF16)


def _retention(zr, c_k, s_k, b, s, t=512, hp=2):
    n = zr.shape[0]
    t = min(t, s)
    nt = s // t
    ng = RET_HEADS // hp
    log_g = jnp.log1p(-(2.0 ** (-5.0 - jnp.arange(RET_HEADS, dtype=F32))))
    col = lambda part: pl.BlockSpec((t, hp * RET_QK), lambda bi, h, ti: (bi * nt + ti, part * ng + h))
    tab = pl.BlockSpec((t, LANE), lambda bi, h, ti: (bi * nt + ti, 0))
    return pl.pallas_call(
        functools.partial(_ret_kernel, t=t, hp=hp),
        grid=(b, ng, nt),
        in_specs=[pl.BlockSpec(memory_space=pltpu.SMEM), col(0), col(1), col(2), col(3), tab, tab],
        out_specs=pl.BlockSpec((t, hp * RET_V), lambda bi, h, ti: (bi * nt + ti, h)),
        out_shape=jax.ShapeDtypeStruct((n, RET_HEADS * RET_V), BF16),
        scratch_shapes=[pltpu.VMEM((hp, RET_QK, RET_V), F32), pltpu.VMEM((hp, t, t), F32)],
        compiler_params=_params("parallel", "parallel", "arbitrary"),
        name="retention",
    )(log_g, zr, zr, zr, zr, c_k, s_k)


def _gmlp_kernel(u_ref, v_ref, lng_ref, lnb_ref, ws_ref, bst_ref, o_ref, *, nblk):
    v = jax.nn.gelu(v_ref[...].astype(F32))
    mu = jnp.mean(v, axis=-1, keepdims=True)
    d = v - mu
    var = jnp.mean(d * d, axis=-1, keepdims=True)
    vn = (d * lax.rsqrt(var + EPS) * lng_ref[...] + lnb_ref[...]).astype(BF16)
    pc_i = lax.broadcasted_iota(jnp.int32, (GM_BLOCK, GM_BLOCK), 0) // CHUNK
    pc_j = lax.broadcasted_iota(jnp.int32, (GM_BLOCK, GM_BLOCK), 1) // CHUNK
    gw = GM_WIDTH // GM_GROUPS
    for g in range(GM_GROUPS):
        w = jnp.where(pc_i >= pc_j, ws_ref[g], 0.0).astype(BF16)
        bias = bst_ref[:, g:g + 1]
        for r in range(nblk):
            rows = slice(r * GM_BLOCK, (r + 1) * GM_BLOCK)
            cols = slice(g * gw, (g + 1) * gw)
            mixed = jnp.dot(w, vn[rows, cols], preferred_element_type=F32) + bias
            u = jax.nn.gelu(u_ref[rows, cols].astype(F32))
            o_ref[rows, cols] = (u * mixed).astype(BF16)


def _gmlp(zg, ln_g, ln_b, w_s, b_s_t, l, nblk=2):
    n = zg.shape[0]
    tm = nblk * GM_BLOCK
    return pl.pallas_call(
        functools.partial(_gmlp_kernel, nblk=nblk),
        grid=(n // tm,),
        in_specs=[
            pl.BlockSpec((tm, GM_WIDTH), lambda i: (i, 0)),
            pl.BlockSpec((tm, GM_WIDTH), lambda i: (i, 1)),
            pl.BlockSpec((1, GM_WIDTH), lambda i: (0, 0)),
            pl.BlockSpec((1, GM_WIDTH), lambda i: (0, 0)),
            pl.BlockSpec((None, GM_GROUPS, GM_BLOCK, GM_BLOCK), lambda i: (l, 0, 0, 0)),
            pl.BlockSpec((GM_BLOCK, GM_GROUPS), lambda i: (0, 0)),
        ],
        out_specs=pl.BlockSpec((tm, GM_WIDTH), lambda i: (i, 0)),
        out_shape=jax.ShapeDtypeStruct((n, GM_WIDTH), BF16),
        compiler_params=_params("parallel"),
        name="gmlp",
    )(zg, zg, ln_g, ln_b, w_s, b_s_t)


def _merge_kernel(h_ref, ya_ref, yb_ref, yc_ref, wg0_ref, wg1_ref, wg2_ref, bg0_ref, bg1_ref, bg2_ref,
                  wb0_ref, wb1_ref, wb2_ref, o_ref):
    h = h_ref[...]

    def branch(y_ref, wg_ref, bg_ref, wb_ref):
        gate = jax.nn.sigmoid(jnp.dot(h, wg_ref[...], preferred_element_type=F32) + bg_ref[...])
        return gate * jnp.dot(y_ref[...], wb_ref[...], preferred_element_type=F32)

    merged = branch(ya_ref, wg0_ref, bg0_ref, wb0_ref)
    merged = merged + branch(yb_ref, wg1_ref, bg1_ref, wb1_ref)
    merged = merged + branch(yc_ref, wg2_ref, bg2_ref, wb2_ref)
    o_ref[...] = merged.astype(BF16)


def _merge(h, ya, yb, yc, w_gate, b_gate, w_br, l, tm=512, tn=512):
    n, d = h.shape
    tm = min(tm, n)
    nj = d // tn
    act = pl.BlockSpec((tm, d), lambda i, j: (i, 0))
    wg = lambda br: pl.BlockSpec((None, d, tn), lambda i, j: (l, 0, br * nj + j))
    bg = lambda br: pl.BlockSpec((1, tn), lambda i, j: (0, br * nj + j))
    wb = lambda br: pl.BlockSpec((None, None, d, tn), lambda i, j: (l, br, 0, j))
    return pl.pallas_call(
        _merge_kernel,
        grid=(n // tm, nj),
        in_specs=[act, act, act, act, wg(0), wg(1), wg(2), bg(0), bg(1), bg(2), wb(0), wb(1), wb(2)],
        out_specs=pl.BlockSpec((tm, tn), lambda i, j: (i, j)),
        out_shape=jax.ShapeDtypeStruct((n, d), BF16),
        compiler_params=_params("parallel", "arbitrary"),
        name="merge",
    )(h, ya, yb, yc, w_gate, w_gate, w_gate, b_gate, b_gate, b_gate, w_br, w_br, w_br)


def _out_proj_kernel(m_ref, w_ref, g_ref, x_ref, o_ref):
    y = jnp.dot(m_ref[...], w_ref[...], preferred_element_type=F32)
    o_ref[...] = x_ref[...] + _rms(y, g_ref[...])


def _out_proj(merged, w_o, g_post, x, l, tm=512):
    n, d = x.shape
    tm = min(tm, n)
    return pl.pallas_call(
        _out_proj_kernel,
        grid=(n // tm,),
        in_specs=[
            pl.BlockSpec((tm, d), lambda i: (i, 0)),
            pl.BlockSpec((None, d, d), lambda i: (l, 0, 0)),
            pl.BlockSpec((1, d), lambda i: (0, 0)),
            pl.BlockSpec((tm, d), lambda i: (i, 0)),
        ],
        out_specs=pl.BlockSpec((tm, d), lambda i: (i, 0)),
        out_shape=jax.ShapeDtypeStruct((n, d), F32),
        compiler_params=_params("parallel"),
        name="out_proj",
    )(merged, w_o, g_post, x)


def _rot_cols(w):
    half = w.shape[-1] // 2
    return jnp.concatenate([-w[..., half:], w[..., :half]], axis=-1)


def _cast_wi_kernel(w_ref, o_ref):
    o_ref[:, :D_FF] = w_ref[...].astype(BF16)
    o_ref[:, D_FF:] = jnp.zeros((o_ref.shape[0], D_FF_PAD - D_FF), BF16)


def _cast_wi(wi, layer, tr=256):
    d = wi.shape[1]
    return pl.pallas_call(
        _cast_wi_kernel,
        grid=(2, d // tr),
        in_specs=[pl.BlockSpec((None, tr, D_FF), lambda h, r: (layer, r, h))],
        out_specs=pl.BlockSpec((None, None, tr, D_FF_PAD), lambda h, r: (0, h, r, 0)),
        out_shape=jax.ShapeDtypeStruct((1, 2, d, D_FF_PAD), BF16),
        compiler_params=_params("parallel", "parallel"),
        name="cast_wi",
    )(wi)


def _cast_wo_kernel(w_ref, o_ref):
    rows = pl.program_id(0) * FF_TILE + lax.broadcasted_iota(jnp.int32, (FF_TILE, 1), 0)
    o_ref[...] = jnp.where(rows < D_FF, w_ref[...], 0.0).astype(BF16)


def _cast_wo(wo, layer):
    d = wo.shape[2]
    return pl.pallas_call(
        _cast_wo_kernel,
        grid=(D_FF_PAD // FF_TILE,),
        in_specs=[pl.BlockSpec((None, FF_TILE, d), lambda j: (layer, j, 0))],
        out_specs=pl.BlockSpec((None, FF_TILE, d), lambda j: (0, j, 0)),
        out_shape=jax.ShapeDtypeStruct((1, D_FF_PAD, d), BF16),
        compiler_params=_params("parallel"),
        name="cast_wo",
    )(wo)


W_IN_SHIFT = MLA_ROPE
W_IN_TILE = 1024


def _cast_w_in_kernel(a_ref, b_ref, o_ref):
    o_ref[...] = jnp.concatenate([a_ref[:, W_IN_SHIFT:], b_ref[:, :W_IN_SHIFT]], axis=1).astype(BF16)


def _cast_w_in(w_in, tr=1024):
    nl, d, _ = w_in.shape
    first = (Q_LORA + KV_LORA) // W_IN_TILE
    per_tile = W_IN_TILE // LANE
    return pl.pallas_call(
        _cast_w_in_kernel,
        grid=(nl, d // tr, (RET_GROUP_W + GM_GROUP_W) // W_IN_TILE),
        in_specs=[pl.BlockSpec((None, tr, W_IN_TILE), lambda l, r, j: (l, r, first + j)),
                  pl.BlockSpec((None, tr, LANE), lambda l, r, j: (l, r, (first + j + 1) * per_tile))],
        out_specs=pl.BlockSpec((None, tr, W_IN_TILE), lambda l, r, j: (l, r, j)),
        out_shape=jax.ShapeDtypeStruct((nl, d, RET_GROUP_W + GM_GROUP_W), BF16),
        compiler_params=_params("parallel", "parallel", "parallel"),
        name="cast_w_in",
    )(w_in, w_in)


def _cast_w_mla_kernel(a_ref, b_ref, o_ref):
    o_kr = Q_LORA + KV_LORA
    o_ref[:, :o_kr] = a_ref[...].astype(BF16)
    w_kr = b_ref[:, :MLA_ROPE]
    o_ref[:, o_kr:] = jnp.concatenate([w_kr, _rot_cols(w_kr)], axis=1).astype(BF16)


def _cast_w_mla(w_in, tr=512):
    nl, d, _ = w_in.shape
    o_kr = Q_LORA + KV_LORA
    return pl.pallas_call(
        _cast_w_mla_kernel,
        grid=(nl, d // tr),
        in_specs=[pl.BlockSpec((None, tr, o_kr), lambda l, r: (l, r, 0)),
                  pl.BlockSpec((None, tr, LANE), lambda l, r: (l, r, o_kr // LANE))],
        out_specs=pl.BlockSpec((None, tr, MLA_GROUP_W), lambda l, r: (l, r, 0)),
        out_shape=jax.ShapeDtypeStruct((nl, d, MLA_GROUP_W), BF16),
        compiler_params=_params("parallel", "parallel"),
        name="cast_w_mla",
    )(w_in, w_in)


def _prep_w_uq(w_uq):
    nl = w_uq.shape[0]
    w = w_uq.reshape(nl, Q_LORA, MLA_HEADS, MLA_NOPE + MLA_ROPE)
    w_rope = w[..., MLA_NOPE:]
    w = jnp.concatenate([w, _rot_cols(w_rope)], axis=-1)
    return w.reshape(nl, Q_LORA, MLA_HEADS * MLA_QK_PAD).astype(BF16)


def _prep_weights(p):
    w = {}
    w_in = p["w_in"].astype(BF16)
    w["w_mla"] = _cast_w_mla(w_in)
    w["w_rg"] = _cast_w_in(w_in)
    w["w_uq"] = _prep_w_uq(p["w_uq"])
    w["gm_b_s_t"] = jnp.swapaxes(p["gm_b_s"], 1, 2)
    return w


def _row(v):
    return v.reshape(1, -1)


def _token_mixer(x, l, p, w, tables, b, s):
    c_r, s_r, c_k, s_k = tables
    zm, h = _norm_mm(x, _row(p["mix_pre_g"][l]), w["w_mla"], l)
    nl = p["w_br"].shape[0]
    w_br_rows = p["w_br"].reshape(nl, N_BRANCH * D_MODEL, D_MODEL)
    zr, (w_gate, w_br) = _mm(h, w["w_rg"], l, 0, RET_GROUP_W, carry=((p["w_gate"], 32), (w_br_rows, 96)))
    zg, (w_o, w_ukv) = _mm(h, w["w_rg"], l, RET_GROUP_W, GM_GROUP_W, carry=((p["w_o"], 64), (p["w_ukv"], 16)))
    w_br = w_br.reshape(1, N_BRANCH, D_MODEL, D_MODEL)
    q, kv, kr = _mla_prep(zm, _row(p["q_norm_g"][l]), _row(p["kv_norm_g"][l]), w["w_uq"], w_ukv, c_r, s_r, l, 0)
    y_a = _attention(q, kv, kr, b, s)
    y_b = _retention(zr, c_k, s_k, b, s)
    y_c = _gmlp(zg, _row(p["gm_ln_g"][l]), _row(p["gm_ln_b"][l]), p["gm_w_s"], w["gm_b_s_t"][l], l)
    merged = _merge(h, y_a, y_b, y_c, w_gate, _row(p["b_gate"][l]), w_br, 0)
    return _out_proj(merged, w_o, _row(p["mix_post_g"][l]), x, 0)


def _trunk(x, pos, p, depth):
    b, s, d = x.shape
    tables = _rope_tables(pos)
    w = _prep_weights(p)
    x = x.reshape(b * s, d)
    w_ffn = (_cast_wi(p["ffn1_wi"], 0), _cast_wo(p["ffn1_wo"], 0))
    for l in range(depth):
        x, w_ffn = _ffn(x, _row(p["ffn1_pre_g"][l]), *w_ffn, _row(p["ffn1_post_g"][l]),
                        nxt=(p["ffn2_wi"], p["ffn2_wo"], l))
        x = _token_mixer(x, l, p, w, tables, b, s)
        nxt = (p["ffn1_wi"], p["ffn1_wo"], l + 1) if l + 1 < depth else None
        x, w_ffn = _ffn(x, _row(p["ffn2_pre_g"][l]), *w_ffn, _row(p["ffn2_post_g"][l]), nxt=nxt)
    return x.reshape(b, s, d)


def kernel(x, pos, ffn1_pre_g, ffn1_wi, ffn1_wo, ffn1_post_g, mix_pre_g, w_in, q_norm_g, w_uq, kv_norm_g, w_ukv, gm_ln_g, gm_ln_b, gm_w_s, gm_b_s, w_gate, b_gate, w_br, w_o, mix_post_g, ffn2_pre_g, ffn2_wi, ffn2_wo, ffn2_post_g):
    p = dict(ffn1_pre_g=ffn1_pre_g, ffn1_wi=ffn1_wi, ffn1_wo=ffn1_wo, ffn1_post_g=ffn1_post_g, mix_pre_g=mix_pre_g,
             w_in=w_in, q_norm_g=q_norm_g, w_uq=w_uq, kv_norm_g=kv_norm_g, w_ukv=w_ukv, gm_ln_g=gm_ln_g,
             gm_ln_b=gm_ln_b, gm_w_s=gm_w_s, gm_b_s=gm_b_s, w_gate=w_gate, b_gate=b_gate, w_br=w_br, w_o=w_o,
             mix_post_g=mix_post_g, ffn2_pre_g=ffn2_pre_g, ffn2_wi=ffn2_wi, ffn2_wo=ffn2_wo, ffn2_post_g=ffn2_post_g)
    return _trunk(x, pos, p, DEPTH)
```
